```python
import jax
import jax.numpy as jnp
from jax import lax
import numpy as np

D_MODEL = 1024
BATCH = 4
SEQ = 4096
DEPTH = 4
DEC_BATCH = 128
DEC_SEQ = 8
PAST_LEN = 8192
PAGE_SIZE = 128

N_GROUPS = 4
D_MIX = D_MODEL
GROUP_W = D_MIX // N_GROUPS
HEAD_DIM = 64
N_HEADS = GROUP_W // HEAD_DIM

W_LORA = 32
A_LORA = 32
G_LORA = 64
RW_COLS = 3 * GROUP_W + W_LORA + A_LORA + G_LORA
RW_GN_EPS = 64e-5

CMP_LEN = 32
CMP_STRIDE = 16
CMP_HID = 128
SLC_BLOCK = 64
N_SELECT = 16
WINDOW = 512
NSA_COLS = GROUP_W + 6 * HEAD_DIM + 3 * N_HEADS

Q_LORA = 128
KV_LORA = 128
QK_NOPE = 64
QK_ROPE = 32
V_HEAD = 64
MLA_COLS = Q_LORA + KV_LORA + QK_ROPE
MLA_SCALE = (QK_NOPE + QK_ROPE) ** -0.5
ROPE_THETA = 10000.0

CONV_W = 31
CONV_COLS = 2 * GROUP_W
CONV_LN_EPS = 1e-5

N_IN = RW_COLS + NSA_COLS + MLA_COLS + CONV_COLS

N_MEM = 256
X_HEADS = 4
X_HEAD_DIM = 64

D_FF = 2816
NORM_EPS = 1e-6
Q_BLOCK = 128
NEG_INF = -1e30
FORCE_SCORE = 1e4

kernel_name = 'hybrid_rwkv7_nsa_mla_conv_decode_step'


def _rmsnorm(x, g):
    xf = x.astype(jnp.float32)
    y = xf * lax.rsqrt(jnp.mean(xf * xf, axis=-1, keepdims=True) + NORM_EPS)
    return (y * g.astype(jnp.float32)).astype(x.dtype)


def _layernorm(x, g, b, eps):
    xf = x.astype(jnp.float32)
    mu = jnp.mean(xf, axis=-1, keepdims=True)
    var = jnp.mean(jnp.square(xf - mu), axis=-1, keepdims=True)
    y = (xf - mu) * lax.rsqrt(var + eps) * g.astype(jnp.float32) + b.astype(jnp.float32)
    return y.astype(x.dtype)


def _masked_softmax(s, mask):
    s = jnp.where(mask, s.astype(jnp.float32), NEG_INF)
    e = jnp.where(mask, jnp.exp(s - jnp.max(s, axis=-1, keepdims=True)), 0.0)
    den = jnp.sum(e, axis=-1, keepdims=True)
    return e / jnp.where(den > 0.0, den, 1.0)


def _swiglu(x, wg, wu, wd):
    return (jax.nn.silu(x @ wg) * (x @ wu)) @ wd


def _rope(x, pos):
    half = QK_ROPE // 2
    inv = ROPE_THETA ** (-jnp.arange(half, dtype=jnp.float32) / half)
    ang = pos.astype(jnp.float32)[:, None] * inv
    ang = ang.reshape(ang.shape[0], *([1] * (x.ndim - 3)), half)
    cos, sin = jnp.cos(ang), jnp.sin(ang)
    xf = x.astype(jnp.float32)
    x1, x2 = xf[..., :half], xf[..., half:]
    return jnp.concatenate([x1 * cos - x2 * sin, x1 * sin + x2 * cos], axis=-1).astype(x.dtype)


def _to_blocks(a):
    b, l = a.shape[:2]
    return jnp.moveaxis(a.reshape(b, l // Q_BLOCK, Q_BLOCK, *a.shape[2:]), 1, 0)


def _from_blocks(a):
    nb, b, qb = a.shape[:3]
    return jnp.moveaxis(a, 0, 1).reshape(b, nb * qb, *a.shape[3:])


def _query_sweep(fn, qs, pos):
    n_q = pos.shape[0]
    if n_q <= Q_BLOCK:
        return fn(qs, pos, 0)
    nb = n_q // Q_BLOCK
    out = lax.map(lambda a: fn(a[0], a[1], a[2]),
                  (tuple(_to_blocks(z) for z in qs), pos.reshape(nb, Q_BLOCK), jnp.arange(nb, dtype=jnp.int32)))
    return _from_blocks(out)


def _gather_pages(pool, layer, page_table):
    rows = pool[layer, page_table]
    return rows.reshape(rows.shape[0], -1, rows.shape[-1])


def _rwkv_mix(p, p_prev, s0, lw):
    b, l, _ = p.shape
    f32 = jnp.float32
    prev = jnp.concatenate([p_prev[:, None].astype(p.dtype), p[:, :-1]], axis=1)
    xs = p + (prev - p) * lw['rw_mu']
    r, k, v, xw, xa, xg = jnp.split(
        xs, [GROUP_W, 2 * GROUP_W, 3 * GROUP_W, 3 * GROUP_W + W_LORA, 3 * GROUP_W + W_LORA + A_LORA], axis=-1)
    w0, a0, k_k, k_a, r_k, ln_w, ln_b = lw['rw_vec'].astype(f32)
    w_log = -jax.nn.softplus(-(w0 + (jnp.tanh(xw) @ lw['rw_w2']).astype(f32))) - 0.5
    decay = jnp.exp(-jnp.exp(w_log))
    a = jax.nn.sigmoid(a0 + (xa @ lw['rw_a2']).astype(f32))
    g = (jax.nn.sigmoid(xg) @ lw['rw_g2']).astype(f32)
    hd = lambda z: z.reshape(b, l, N_HEADS, HEAD_DIM)
    kf = k.astype(f32)
    kk = hd(kf * k_k)
    kk = kk / jnp.maximum(jnp.sqrt(jnp.sum(kk * kk, axis=-1, keepdims=True)), 1e-12)
    k_eff = hd(kf * (1.0 + (a - 1.0) * k_a))
    rh, vh, wh, ah = hd(r.astype(f32)), hd(v.astype(f32)), hd(decay), hd(a)

    def step(s, inp):
        r_t, w_t, k_t, v_t, kk_t, a_t = inp
        sa = jnp.einsum('bhvk,bhk->bhv', s, -kk_t)
        s = (s * w_t[:, :, None, :] + sa[..., None] * (kk_t * a_t)[:, :, None, :]
             + v_t[..., None] * k_t[:, :, None, :])
        return s, jnp.einsum('bhvk,bhk->bhv', s, r_t)

    tm = lambda z: jnp.moveaxis(z, 1, 0)
    s_fin, o = lax.scan(step, s0.astype(f32), (tm(rh), tm(wh), tm(k_eff), tm(vh), tm(kk), tm(ah)))
    o = jnp.moveaxis(o, 0, 1)
    mu = jnp.mean(o, axis=-1, keepdims=True)
    var = jnp.mean(jnp.square(o - mu), axis=-1, keepdims=True)
    o = ((o - mu) * lax.rsqrt(var + RW_GN_EPS)).reshape(b, l, GROUP_W) * ln_w + ln_b
    bonus = jnp.sum(rh * k_eff * r_k.reshape(N_HEADS, HEAD_DIM), axis=-1, keepdims=True) * vh
    out = (o + bonus.reshape(b, l, GROUP_W)) * g
    return out.astype(p.dtype), p[:, -1], s_fin.astype(s0.dtype)


def _conv_mix(u, buf, lw):
    val, gate = jnp.split(u, 2, axis=-1)
    h = val * jax.nn.sigmoid(gate)
    hp = jnp.concatenate([buf.astype(h.dtype), h], axis=1)
    y = lax.conv_general_dilated(hp, lw['conv_dw'][:, None, :].astype(h.dtype), window_strides=(1,),
                                 padding='VALID', dimension_numbers=('NWC', 'WIO', 'NWC'),
                                 feature_group_count=GROUP_W)
    dw_b, ln_w, ln_b = lw['conv_vec']
    y = _layernorm(y + dw_b, ln_w, ln_b, CONV_LN_EPS)
    y = jax.nn.silu(y) @ lw['conv_pw']
    return y, hp[:, -(CONV_W - 1):]


def _latent_attend(q, t, keys, vals, kpos):
    s = jnp.einsum('bqhc,btc->bhqt', q, keys) * MLA_SCALE
    pr = _masked_softmax(s, (kpos[None, :] <= t[:, None])[None, None])
    return jnp.einsum('bhqt,btc->bqhc', pr.astype(vals.dtype), vals)


def _mla_mix(p, pos, past_rows, lw):
    b, l, _ = p.shape
    c_q, c_kv, k_rope = jnp.split(p, [Q_LORA, Q_LORA + KV_LORA], axis=-1)
    q = (_rmsnorm(c_q, lw['mla_q_norm']) @ lw['mla_w_qb']).reshape(b, l, N_HEADS, QK_NOPE + QK_ROPE)
    q_lat = jnp.einsum('blhd,chd->blhc', q[..., :QK_NOPE], lw['mla_w_kb'])
    q_full = jnp.concatenate([q_lat, _rope(q[..., QK_NOPE:], pos)], axis=-1)
    new_rows = jnp.concatenate([_rmsnorm(c_kv, lw['mla_kv_norm']), _rope(k_rope, pos)], axis=-1)
    keys = new_rows if past_rows is None else jnp.concatenate([past_rows.astype(p.dtype), new_rows], axis=1)
    kpos = jnp.arange(keys.shape[1], dtype=jnp.int32)
    vals = keys[..., :KV_LORA]
    o_lat = _query_sweep(lambda qs, t, blk: _latent_attend(qs[0], t, keys, vals, kpos), (q_full,), pos)
    o = jnp.einsum('blhc,chd->blhd', o_lat, lw['mla_w_vb']).reshape(b, l, N_HEADS * V_HEAD)
    return o, new_rows


def _compress(rows, pos_emb, w1, b1, w2, b2):
    b, t, d = rows.shape
    n_ch = t // CMP_STRIDE
    ch = rows[:, :n_ch * CMP_STRIDE].reshape(b, n_ch, CMP_STRIDE * d)
    half = CMP_STRIDE * d
    h = ch[:, :-1] @ w1[:half] + ch[:, 1:] @ w1[half:] + (pos_emb.reshape(-1) @ w1 + b1)
    return jax.nn.gelu(h) @ w2 + b2


def _nsa_block(q, g, t, kc, vc, kc_end, ks, vs, overlap, kw, vw, wpos):
    b, nq = q.shape[:2]
    scale = HEAD_DIM ** -0.5
    s = jnp.einsum('bqhd,bnd->bhqn', q, kc) * scale
    p_c = _masked_softmax(s, (kc_end[None, :] <= t[:, None] + 1)[None, None])
    o_c = jnp.einsum('bhqn,bnd->bqhd', p_c.astype(vc.dtype), vc)
    imp = jnp.einsum('bhqn,nj->bqj', p_c, overlap)
    n_sb = overlap.shape[1]
    j = jnp.arange(n_sb)[None, :]
    cur = (t // SLC_BLOCK)[:, None]
    forced = (j == 0) | (j == cur) | (j == cur - 1)
    score = jnp.where(forced, FORCE_SCORE, jnp.where(j <= cur, imp, -1.0))
    _, sel = lax.top_k(score, min(N_SELECT, n_sb))
    idx = (sel[..., None] * SLC_BLOCK + jnp.arange(SLC_BLOCK)).reshape(b, nq, -1)
    kg = jax.vmap(lambda a, i: a[i])(ks, idx)
    vg = jax.vmap(lambda a, i: a[i])(vs, idx)
    s = jnp.einsum('bqhd,bqkd->bhqk', q, kg) * scale
    p_s = _masked_softmax(s, (idx <= t[None, :, None])[:, None])
    o_s = jnp.einsum('bhqk,bqkd->bqhd', p_s.astype(vg.dtype), vg)
    s = jnp.einsum('bqhd,bkd->bhqk', q, kw) * scale
    wm = (wpos[None, :] <= t[:, None]) & (wpos[None, :] > t[:, None] - WINDOW) & (wpos[None, :] >= 0)
    p_w = _masked_softmax(s, wm[None, None])
    o_w = jnp.einsum('bhqk,bkd->bqhd', p_w.astype(vw.dtype), vw)
    return g[..., 0:1] * o_c + g[..., 1:2] * o_s + g[..., 2:3] * o_w


def _nsa_mix(p, pos, past_rows, win_buf, lw):
    b, l, _ = p.shape
    dt = p.dtype
    q, kv, g_logit = jnp.split(p, [GROUP_W, GROUP_W + 6 * HEAD_DIM], axis=-1)
    q = q.reshape(b, l, N_HEADS, HEAD_DIM)
    gates = jax.nn.sigmoid(g_logit.astype(jnp.float32)).reshape(b, l, N_HEADS, 3).astype(dt)
    new_rows, new_win = kv[..., :4 * HEAD_DIM], kv[..., 4 * HEAD_DIM:]
    rows = new_rows if past_rows is None else jnp.concatenate([past_rows.astype(dt), new_rows], axis=1)
    t_all = rows.shape[1]
    k_cmp, v_cmp, k_slc, v_slc = jnp.split(rows, 4, axis=-1)
    kc = _compress(k_cmp, lw['cmp_pos'][0], lw['cmp_w1'][0], lw['cmp_b1'][0], lw['cmp_w2'][0], lw['cmp_b2'][0])
    vc = _compress(v_cmp, lw['cmp_pos'][1], lw['cmp_w1'][1], lw['cmp_b1'][1], lw['cmp_w2'][1], lw['cmp_b2'][1])
    n_cb = kc.shape[1]
    kc_end = CMP_STRIDE * jnp.arange(n_cb, dtype=jnp.int32) + CMP_LEN
    n_sb = -(-t_all // SLC_BLOCK)
    pad = ((0, 0), (0, n_sb * SLC_BLOCK - t_all), (0, 0))
    k_slc, v_slc = jnp.pad(k_slc, pad), jnp.pad(v_slc, pad)
    ci = jnp.arange(n_cb)[:, None]
    sj = jnp.arange(n_sb)[None, :]
    overlap = ((CMP_STRIDE * ci < SLC_BLOCK * (sj + 1)) &
               (CMP_STRIDE * ci + CMP_LEN > SLC_BLOCK * sj)).astype(jnp.float32)
    if win_buf is None:
        win = jnp.pad(new_win, ((0, 0), (WINDOW, 0), (0, 0)))
        win_pos = jnp.arange(-WINDOW, l, dtype=jnp.int32)
        keep = min(WINDOW, l)
    else:
        win = jnp.concatenate([win_buf.astype(dt), new_win], axis=1)
        win_pos = pos[0] - win_buf.shape[1] + jnp.arange(win.shape[1], dtype=jnp.int32)
        keep = win_buf.shape[1]
    span = min(win.shape[1], WINDOW + Q_BLOCK)

    def block(qs, t, blk):
        w_b = lax.dynamic_slice_in_dim(win, blk * Q_BLOCK, span, axis=1)
        wp = lax.dynamic_slice_in_dim(win_pos, blk * Q_BLOCK, span)
        return _nsa_block(qs[0], qs[1], t, kc, vc, kc_end, k_slc, v_slc, overlap,
                          w_b[..., :HEAD_DIM], w_b[..., HEAD_DIM:], wp)

    o = _query_sweep(block, (q, gates), pos).reshape(b, l, GROUP_W)
    return o, new_rows, win[:, -keep:]


def _cross_attend(h, mem_kv, lw):
    b, l, _ = h.shape
    q = (h @ lw['x_wq']).reshape(b, l, X_HEADS, X_HEAD_DIM)
    k, v = jnp.split(mem_kv.astype(h.dtype), 2, axis=-1)
    k = k.reshape(b, -1, X_HEADS, X_HEAD_DIM)
    v = v.reshape(b, -1, X_HEADS, X_HEAD_DIM)
    s = jnp.einsum('blhd,bmhd->bhlm', q, k) * (X_HEAD_DIM ** -0.5)
    pr = jax.nn.softmax(s.astype(jnp.float32), axis=-1).astype(h.dtype)
    o = jnp.einsum('bhlm,bmhd->blhd', pr, v).reshape(b, l, X_HEADS * X_HEAD_DIM)
    return o @ lw['x_wo']


def _layer(x, lw, mem_kv, pos, past_mla, past_nsa, win_buf, rw_prev, rw_state, conv_buf):
    n = lw['norms']
    x = x + 0.5 * _swiglu(_rmsnorm(x, n[0]), lw['ffn_w_gate'][0], lw['ffn_w_up'][0], lw['ffn_w_down'][0])
    proj = _rmsnorm(x, n[1]) @ lw['w_in']
    p_rw, p_nsa, p_mla, p_conv = jnp.split(
        proj, [RW_COLS, RW_COLS + NSA_COLS, RW_COLS + NSA_COLS + MLA_COLS], axis=-1)
    o_rw, rw_last, rw_state = _rwkv_mix(p_rw, rw_prev, rw_state, lw)
    o_nsa, nsa_rows, win_state = _nsa_mix(p_nsa, pos, past_nsa, win_buf, lw)
    o_mla, mla_rows = _mla_mix(p_mla, pos, past_mla, lw)
    o_conv, conv_state = _conv_mix(p_conv, conv_buf, lw)
    x = x + jnp.concatenate([o_rw, o_nsa, o_mla, o_conv], axis=-1) @ lw['w_out']
    x = x + _cross_attend(_rmsnorm(x, n[2]), mem_kv, lw)
    x = x + 0.5 * _swiglu(_rmsnorm(x, n[4]), lw['ffn_w_gate'][1], lw['ffn_w_up'][1], lw['ffn_w_down'][1])
    return x, (mla_rows, nsa_rows, win_state, rw_last, rw_state, conv_state)


def _stack(states, i):
    return jnp.stack([s[i] for s in states])


def setup_inputs(seed: int = 0) -> dict:
    key = jax.random.key(seed)
    keys = iter(jax.random.split(key, 64))

    def nrm(shape, scale=1.0):
        return scale * jax.random.normal(next(keys), shape, jnp.float32)

    n_pages = PAST_LEN // PAGE_SIZE
    n_used = DEC_BATCH * n_pages
    n_pool = n_used + n_used // 4
    w_buf = min(WINDOW, PAST_LEN)
    page_table = jax.random.permutation(next(keys), n_pool)[:n_used].reshape(DEC_BATCH, n_pages).astype(jnp.int32)
    gw = (DEPTH, GROUP_W)
    rw_vec = jnp.stack([nrm(gw, 0.5), nrm(gw, 0.1), 0.85 + nrm(gw, 0.05), 1.0 + nrm(gw, 0.05),
                        nrm(gw, 0.1), 1.0 + nrm(gw, 0.05), nrm(gw, 0.01)], axis=1)
    conv_vec = jnp.stack([nrm(gw, 0.01), 1.0 + nrm(gw, 0.05), nrm(gw, 0.01)], axis=1)
    return {
        'x_prompt': nrm((BATCH, SEQ, D_MODEL)),
        'x_sample': nrm((DEC_BATCH, DEC_SEQ, D_MODEL)),
        'cache_mla': nrm((DEPTH, n_pool, PAGE_SIZE, KV_LORA + QK_ROPE)),
        'cache_nsa': nrm((DEPTH, n_pool, PAGE_SIZE, 4 * HEAD_DIM)),
        'cache_nsa_win': nrm((DEPTH, DEC_BATCH, w_buf, 2 * HEAD_DIM)),
        'cache_mem': nrm((DEPTH, DEC_BATCH, N_MEM, 2 * X_HEADS * X_HEAD_DIM)),
        'state_rwkv': nrm((DEPTH, DEC_BATCH, N_HEADS, HEAD_DIM, HEAD_DIM), 0.5),
        'state_rwkv_shift': nrm((DEPTH, DEC_BATCH, RW_COLS)),
        'state_conv': nrm((DEPTH, DEC_BATCH, CONV_W - 1, GROUP_W), 0.5),
        'page_table': page_table,
        'mem_prompt': nrm((BATCH, N_MEM, D_MODEL)),
        'norms': 1.0 + nrm((DEPTH, 5, D_MODEL), 0.05),
        'ffn_w_gate': nrm((DEPTH, 2, D_MODEL, D_FF), D_MODEL ** -0.5),
        'ffn_w_up': nrm((DEPTH, 2, D_MODEL, D_FF), D_MODEL ** -0.5),
        'ffn_w_down': nrm((DEPTH, 2, D_FF, D_MODEL), D_FF ** -0.5),
        'w_in': nrm((DEPTH, D_MODEL, N_IN), D_MODEL ** -0.5),
        'w_out': nrm((DEPTH, D_MIX, D_MODEL), D_MIX ** -0.5),
        'rw_mu': jax.random.uniform(next(keys), (DEPTH, RW_COLS), jnp.float32),
        'rw_vec': rw_vec,
        'rw_w2': nrm((DEPTH, W_LORA, GROUP_W), 0.1),
        'rw_a2': nrm((DEPTH, A_LORA, GROUP_W), A_LORA ** -0.5),
        'rw_g2': nrm((DEPTH, G_LORA, GROUP_W), G_LORA ** -0.5),
        'cmp_pos': nrm((DEPTH, 2, CMP_LEN, HEAD_DIM), 0.5),
        'cmp_w1': nrm((DEPTH, 2, CMP_LEN * HEAD_DIM, CMP_HID), (CMP_LEN * HEAD_DIM) ** -0.5),
        'cmp_b1': nrm((DEPTH, 2, CMP_HID), 0.01),
        'cmp_w2': nrm((DEPTH, 2, CMP_HID, HEAD_DIM), CMP_HID ** -0.5),
        'cmp_b2': nrm((DEPTH, 2, HEAD_DIM), 0.01),
        'mla_q_norm': 1.0 + nrm((DEPTH, Q_LORA), 0.05),
        'mla_kv_norm': 1.0 + nrm((DEPTH, KV_LORA), 0.05),
        'mla_w_qb': nrm((DEPTH, Q_LORA, N_HEADS * (QK_NOPE + QK_ROPE)), Q_LORA ** -0.5),
        'mla_w_kb': nrm((DEPTH, KV_LORA, N_HEADS, QK_NOPE), KV_LORA ** -0.5),
        'mla_w_vb': nrm((DEPTH, KV_LORA, N_HEADS, V_HEAD), KV_LORA ** -0.5),
        'conv_dw': nrm((DEPTH, CONV_W, GROUP_W), CONV_W ** -0.5),
        'conv_vec': conv_vec,
        'conv_pw': nrm((DEPTH, GROUP_W, GROUP_W), GROUP_W ** -0.5),
        'x_wq': nrm((DEPTH, D_MODEL, X_HEADS * X_HEAD_DIM), D_MODEL ** -0.5),
        'x_wkv': nrm((DEPTH, D_MODEL, 2 * X_HEADS * X_HEAD_DIM), D_MODEL ** -0.5),
        'x_wo': nrm((DEPTH, X_HEADS * X_HEAD_DIM, D_MODEL), (X_HEADS * X_HEAD_DIM) ** -0.5),
        'final_norm': 1.0 + nrm((D_MODEL,), 0.05),
    }


def reference(x_prompt, x_sample, cache_mla, cache_nsa, cache_nsa_win, cache_mem, state_rwkv,
              state_rwkv_shift, state_conv, page_table, mem_prompt, norms, ffn_w_gate, ffn_w_up,
              ffn_w_down, w_in, w_out, rw_mu, rw_vec, rw_w2, rw_a2, rw_g2, cmp_pos, cmp_w1, cmp_b1,
              cmp_w2, cmp_b2, mla_q_norm, mla_kv_norm, mla_w_qb, mla_w_kb, mla_w_vb, conv_dw, conv_vec,
              conv_pw, x_wq, x_wkv, x_wo, final_norm):
    dt = x_prompt.dtype
    bp = x_prompt.shape[0]
    pos_p = jnp.arange(x_prompt.shape[1], dtype=jnp.int32)
    pos_s = PAST_LEN + jnp.arange(x_sample.shape[1], dtype=jnp.int32)
    xp, xs = x_prompt, x_sample
    sp, ss = [], []
    for l in range(DEPTH):
        lw = dict(norms=norms[l], ffn_w_gate=ffn_w_gate[l], ffn_w_up=ffn_w_up[l], ffn_w_down=ffn_w_down[l],
                  w_in=w_in[l], w_out=w_out[l], rw_mu=rw_mu[l], rw_vec=rw_vec[l], rw_w2=rw_w2[l],
                  rw_a2=rw_a2[l], rw_g2=rw_g2[l], cmp_pos=cmp_pos[l], cmp_w1=cmp_w1[l], cmp_b1=cmp_b1[l],
                  cmp_w2=cmp_w2[l], cmp_b2=cmp_b2[l], mla_q_norm=mla_q_norm[l], mla_kv_norm=mla_kv_norm[l],
                  mla_w_qb=mla_w_qb[l], mla_w_kb=mla_w_kb[l], mla_w_vb=mla_w_vb[l], conv_dw=conv_dw[l],
                  conv_vec=conv_vec[l], conv_pw=conv_pw[l], x_wq=x_wq[l], x_wo=x_wo[l])
        mem_kv = _rmsnorm(mem_prompt, norms[l, 3]) @ x_wkv[l]
        xp, st = _layer(xp, lw, mem_kv, pos_p, None, None, None,
                        jnp.zeros((bp, RW_COLS), dt),
                        jnp.zeros((bp, N_HEADS, HEAD_DIM, HEAD_DIM), jnp.float32),
                        jnp.zeros((bp, CONV_W - 1, GROUP_W), dt))
        sp.append(st + (mem_kv,))
        xs, st = _layer(xs, lw, cache_mem[l], pos_s,
                        _gather_pages(cache_mla, l, page_table), _gather_pages(cache_nsa, l, page_table),
                        cache_nsa_win[l], state_rwkv_shift[l], state_rwkv[l], state_conv[l])
        ss.append(st)
    y_prompt = _rmsnorm(xp, final_norm)
    y_sample = _rmsnorm(xs, final_norm)
    return (y_prompt, y_sample,
            _stack(sp, 0), _stack(sp, 1), _stack(sp, 2), _stack(sp, 6), _stack(sp, 4), _stack(sp, 3), _stack(sp, 5),
            _stack(ss, 0), _stack(ss, 1), _stack(ss, 2), _stack(ss, 4), _stack(ss, 3), _stack(ss, 5))
```

```python
import functools

import jax
import jax.numpy as jnp
from jax import lax
from jax.experimental import pallas as pl
from jax.experimental.pallas import tpu as pltpu

F32 = jnp.float32
BF16 = jnp.bfloat16
HIGHEST = lax.Precision.HIGHEST

GROUP_W = 256
HEAD_DIM = 64
N_HEADS = 4
LANES = 128
SUBLANES = 8
W_LORA, A_LORA, G_LORA = 32, 32, 64
RW_COLS = 3 * GROUP_W + W_LORA + A_LORA + G_LORA
RW_GN_EPS = 64e-5
CMP_LEN, CMP_STRIDE, CMP_HID = 32, 16, 128
SLC_BLOCK, N_SELECT, WINDOW = 64, 16, 512
KV_LORA, QK_NOPE, QK_ROPE = 128, 64, 32
C_MLA = KV_LORA + QK_ROPE
MLA_SCALE = (QK_NOPE + QK_ROPE) ** -0.5
ROPE_THETA = 10000.0
CONV_W = 31
CONV_LN_EPS = 1e-5
NORM_EPS = 1e-6
NEG = -1e30
FORCE_SCORE = 1e4
NSA_W = 768
MLA_W = 512
VMEM_LIMIT = 56 * 1024 * 1024


def _cparams(sem, vmem=VMEM_LIMIT):
    return pltpu.CompilerParams(dimension_semantics=sem, vmem_limit_bytes=vmem)


def _rms(x, g):
    return x * lax.rsqrt(jnp.mean(x * x, axis=-1, keepdims=True) + NORM_EPS) * g


def _bdot(a, b):
    return jnp.dot(a.astype(BF16), b.astype(BF16), preferred_element_type=F32)


def _bdot_t(a, b):
    return lax.dot_general(a.astype(BF16), b.astype(BF16), (((1,), (1,)), ((), ())), preferred_element_type=F32)


def _seg_ones(n, seg):
    r = lax.broadcasted_iota(jnp.int32, (n, n), 0) // seg
    c = lax.broadcasted_iota(jnp.int32, (n, n), 1) // seg
    return (r == c).astype(F32)


def _full(shape):
    nd = len(shape)
    return pl.BlockSpec(shape, lambda *_: (0,) * nd)


def _swiglu_to(xn, wg_ref, wu_ref, h_ref, chunk):
    for c in range(h_ref.shape[1] // chunk):
        sl = slice(c * chunk, (c + 1) * chunk)
        gate = jnp.dot(xn, wg_ref[:, sl], preferred_element_type=F32)
        up = jnp.dot(xn, wu_ref[:, sl], preferred_element_type=F32)
        h_ref[:, sl] = (gate * jax.nn.sigmoid(gate) * up).astype(BF16)


def _ffn_a_kernel(x_ref, g_ref, wg_ref, wu_ref, h_ref, *, chunk):
    xn = _rms(x_ref[...], g_ref[...]).astype(BF16)
    _swiglu_to(xn, wg_ref, wu_ref, h_ref, chunk)


def _ffn_b_proj_kernel(x_ref, h_ref, wd_ref, g_ref, win_ref, x1_ref, prw_ref, pnsa_ref, pmla_ref, pconv_ref):
    x1 = x_ref[...] + 0.5 * jnp.dot(h_ref[...], wd_ref[...], preferred_element_type=F32)
    x1_ref[...] = x1
    xn = _rms(x1, g_ref[...]).astype(BF16)
    off = 0
    for ref in (prw_ref, pnsa_ref, pmla_ref, pconv_ref):
        w = ref.shape[1]
        ref[...] = jnp.dot(xn, win_ref[:, off:off + w], preferred_element_type=F32)
        off += w


def _mix_out_kernel(x_ref, orw_ref, onsa_ref, omla_ref, oconv_ref, wo_ref, g_ref, wq_ref, x2_ref, q_ref):
    acc = x_ref[...]
    for i, ref in enumerate((orw_ref, onsa_ref, omla_ref, oconv_ref)):
        acc = acc + _bdot(ref[...], wo_ref[i * GROUP_W:(i + 1) * GROUP_W, :])
    x2_ref[...] = acc
    q_ref[...] = _bdot(_rms(acc, g_ref[...]), wq_ref[...]) * (HEAD_DIM ** -0.5)


def _xo_ffn_a_kernel(x_ref, ox_ref, wo_ref, g_ref, wg_ref, wu_ref, x3_ref, h_ref, *, chunk):
    x3 = x_ref[...] + _bdot(ox_ref[...], wo_ref[...])
    x3_ref[...] = x3
    _swiglu_to(_rms(x3, g_ref[...]).astype(BF16), wg_ref, wu_ref, h_ref, chunk)


def _ffn_b_kernel(x_ref, h_ref, wd_ref, o_ref):
    o_ref[...] = x_ref[...] + 0.5 * jnp.dot(h_ref[...], wd_ref[...], preferred_element_type=F32)


def _norm_kernel(x_ref, g_ref, o_ref):
    o_ref[...] = _rms(x_ref[...], g_ref[...])


def _norm_proj_kernel(x_ref, g_ref, w_ref, o_ref):
    o_ref[...] = _bdot(_rms(x_ref[...], g_ref[...]), w_ref[...])


def _token_call(body, tiled_in, full_in, out_widths, out_dtypes, tm):
    m = tiled_in[0].shape[0]
    assert m % tm == 0
    in_specs = [pl.BlockSpec((tm, a.shape[1]), lambda i: (i, 0)) for a in tiled_in]
    in_specs += [_full(a.shape) for a in full_in]
    out_specs = [pl.BlockSpec((tm, w), lambda i: (i, 0)) for w in out_widths]
    out_shape = [jax.ShapeDtypeStruct((m, w), dt) for w, dt in zip(out_widths, out_dtypes)]
    return pl.pallas_call(body, grid=(m // tm,), in_specs=in_specs, out_specs=out_specs, out_shape=out_shape,
                          compiler_params=_cparams(("parallel",)))(*tiled_in, *full_in)


def _ff_chunk(d_ff):
    return 256 if d_ff % 256 == 0 else LANES


def _xattn_kernel(q_ref, kv_ref, o_ref, *, groups, lq):
    for g in range(groups):
        rows = slice(g * lq, (g + 1) * lq)
        for h in range(N_HEADS):
            cols = slice(h * HEAD_DIM, (h + 1) * HEAD_DIM)
            k = kv_ref[g, :, cols]
            v = kv_ref[g, :, GROUP_W + h * HEAD_DIM:GROUP_W + (h + 1) * HEAD_DIM]
            s = _bdot_t(q_ref[rows, cols], k)
            e = jnp.exp(s - jnp.max(s, axis=-1, keepdims=True))
            p = e / jnp.sum(e, axis=-1, keepdims=True)
            o_ref[rows, cols] = _bdot(p, v)


def _cross_attend(q, mem_kv, groups, lq_tile):
    b, n_mem, _ = mem_kv.shape
    l = q.shape[0] // b
    if groups > 1:
        assert lq_tile == l and b % groups == 0
        grid = (b // groups,)
        q_spec = pl.BlockSpec((groups * l, GROUP_W), lambda i: (i, 0))
        kv_spec = pl.BlockSpec((groups, n_mem, 2 * GROUP_W), lambda i: (i, 0, 0))
    else:
        assert l % lq_tile == 0
        nq = l // lq_tile
        grid = (b, nq)
        q_spec = pl.BlockSpec((lq_tile, GROUP_W), lambda bi, i: (bi * nq + i, 0))
        kv_spec = pl.BlockSpec((1, n_mem, 2 * GROUP_W), lambda bi, i: (bi, 0, 0))
    return pl.pallas_call(
        functools.partial(_xattn_kernel, groups=groups, lq=lq_tile), grid=grid, in_specs=[q_spec, kv_spec],
        out_specs=q_spec, out_shape=jax.ShapeDtypeStruct(q.shape, F32),
        compiler_params=_cparams(("parallel",) * len(grid)))(q, mem_kv)


def _rw_prep_kernel(p_ref, prev_ref, mu_ref, vec_ref, w2_ref, a2_ref, g2_ref,
                    rp_ref, nkk_ref, w_ref, kka_ref, k_ref, v_ref, ovk_ref, g_ref, bonus_ref, carry_ref):
    j = pl.program_id(1)

    @pl.when(j == 0)
    def _():
        carry_ref[...] = prev_ref[...]

    p = p_ref[...]
    tl = p.shape[0]
    row = lax.broadcasted_iota(jnp.int32, p.shape, 0)
    prev = jnp.where(row == 0, carry_ref[...], pltpu.roll(p, 1, axis=0))
    carry_ref[...] = p[tl - 1:tl, :]
    xs = p + (prev - p) * mu_ref[...]
    r, k, v = xs[:, 0:GROUP_W], xs[:, GROUP_W:2 * GROUP_W], xs[:, 2 * GROUP_W:3 * GROUP_W]
    o = 3 * GROUP_W
    xw, xa, xg = xs[:, o:o + W_LORA], xs[:, o + W_LORA:o + W_LORA + A_LORA], xs[:, o + W_LORA + A_LORA:]
    w0, a0, k_k, k_a, r_k = (vec_ref[i:i + 1, :] for i in range(5))
    z = -(w0 + _bdot(jnp.tanh(xw), w2_ref[...]))
    softplus = jnp.maximum(z, 0.0) + jnp.log1p(jnp.exp(-jnp.abs(z)))
    decay = jnp.exp(-jnp.exp(-softplus - 0.5))
    a = jax.nn.sigmoid(a0 + _bdot(xa, a2_ref[...]))
    g_ref[...] = _bdot(jax.nn.sigmoid(xg), g2_ref[...])
    ones = _seg_ones(GROUP_W, HEAD_DIM)
    seg = lambda t: jnp.dot(t, ones, precision=HIGHEST, preferred_element_type=F32)
    kk = k * k_k
    kk = kk / jnp.maximum(jnp.sqrt(seg(kk * kk)), 1e-12)
    k_eff = k * (1.0 + (a - 1.0) * k_a)
    nkk = -kk
    kka = kk * a
    rp_ref[...] = decay * r + nkk * seg(kka * r)
    ovk_ref[...] = v * seg(k_eff * r)
    bonus_ref[...] = seg(r * k_eff * r_k) * v
    nkk_ref[...] = nkk
    w_ref[...] = decay
    kka_ref[...] = kka
    k_ref[...] = k_eff
    v_ref[...] = v


def _rw_scan_kernel(rp_ref, nkk_ref, w_ref, kka_ref, k_ref, v_ref, ovk_ref, g_ref, bonus_ref, s0_ref, vec_ref,
                    out_ref, sfin_ref, s_ref, o_ref, *, nb, tc):
    j = pl.program_id(1)

    @pl.when(j == 0)
    def _():
        s_ref[...] = s0_ref[...]

    lane = lax.broadcasted_iota(jnp.int32, (1, LANES), 1)
    left1 = lane < HEAD_DIM
    left = lax.broadcasted_iota(jnp.int32, (HEAD_DIM, LANES), 1) < HEAD_DIM
    eye2 = (lax.broadcasted_iota(jnp.int32, (HEAD_DIM, LANES), 0)
            == lax.broadcasted_iota(jnp.int32, (HEAD_DIM, LANES), 1) % HEAD_DIM).astype(F32)
    rsum = lambda t: jnp.sum(t, axis=1, keepdims=True)

    def halves(rw):
        return jnp.where(left1, rw, 0.0), jnp.where(left1, 0.0, rw)

    def step(t8, carry):
        base = pl.multiple_of(t8 * SUBLANES, SUBLANES)
        for b in range(nb):
            for p in range(N_HEADS // 2):
                sl = slice(p * LANES, (p + 1) * LANES)
                ld = lambda ref: ref[b, pl.ds(base, SUBLANES), sl]
                nkk, rp, v, w, kka, k = (ld(r) for r in (nkk_ref, rp_ref, v_ref, w_ref, kka_ref, k_ref))
                s = s_ref[b, p]
                o_rows = []
                for i in range(SUBLANES):
                    row = lambda x: x[i:i + 1, :]
                    nkk_l, nkk_r = halves(row(nkk))
                    rp_l, rp_r = halves(row(rp))
                    v_l, v_r = halves(row(v))
                    sa = jnp.where(left, rsum(s * nkk_l), rsum(s * nkk_r))
                    oc = jnp.where(left, rsum(s * rp_l), rsum(s * rp_r))
                    vc = jnp.where(left, rsum(eye2 * v_l), rsum(eye2 * v_r))
                    s = s * row(w) + sa * row(kka) + vc * row(k)
                    o_rows.append(jnp.sum(oc * eye2, axis=0, keepdims=True))
                s_ref[b, p] = s
                o_ref[b, pl.ds(base, SUBLANES), sl] = jnp.concatenate(o_rows, axis=0) + ld(ovk_ref)
        return carry

    lax.fori_loop(0, tc // SUBLANES, step, 0)

    @pl.when(j == pl.num_programs(1) - 1)
    def _():
        sfin_ref[...] = s_ref[...]

    mean = _seg_ones(GROUP_W, HEAD_DIM) * (1.0 / HEAD_DIM)
    seg_mean = lambda t: jnp.dot(t, mean, precision=HIGHEST, preferred_element_type=F32)
    ln_w, ln_b = vec_ref[5:6, :], vec_ref[6:7, :]
    for b in range(nb):
        o = o_ref[b]
        d = o - seg_mean(o)
        on = d * lax.rsqrt(seg_mean(d * d) + RW_GN_EPS) * ln_w + ln_b
        out_ref[b] = (on + bonus_ref[b]) * g_ref[b]


def _rwkv_mix(p, p_prev, s0, lw, nb, tc, tl):
    b = p_prev.shape[0]
    l = p.shape[0] // b
    assert l % tl == 0 and l % tc == 0 and b % nb == 0 and tc % SUBLANES == 0
    nt = l // tl
    tile = pl.BlockSpec((tl, GROUP_W), lambda bi, j: (bi * nt + j, 0))
    outs = pl.pallas_call(
        _rw_prep_kernel, grid=(b, nt),
        in_specs=[pl.BlockSpec((tl, RW_COLS), lambda bi, j: (bi * nt + j, 0)),
                  pl.BlockSpec((None, 1, RW_COLS), lambda bi, j: (bi, 0, 0)),
                  _full((1, RW_COLS)), _full((7, GROUP_W)), _full((W_LORA, GROUP_W)), _full((A_LORA, GROUP_W)),
                  _full((G_LORA, GROUP_W))],
        out_specs=[tile] * 9, out_shape=[jax.ShapeDtypeStruct((b * l, GROUP_W), F32)] * 9,
        scratch_shapes=[pltpu.VMEM((1, RW_COLS), F32)],
        compiler_params=_cparams(("arbitrary", "arbitrary")))(
            p, p_prev[:, None, :], lw['rw_mu'], lw['rw_vec'], lw['rw_w2'], lw['rw_a2'], lw['rw_g2'])
    seqs = [a.reshape(b, l, GROUP_W) for a in outs]
    s0p = s0.reshape(b, 2, 2, HEAD_DIM, HEAD_DIM).transpose(0, 1, 3, 2, 4).reshape(b, 2, HEAD_DIM, LANES)
    seq_spec = pl.BlockSpec((nb, tc, GROUP_W), lambda bi, j: (bi, j, 0))
    st_spec = pl.BlockSpec((nb, 2, HEAD_DIM, LANES), lambda bi, j: (bi, 0, 0, 0))
    out, sfin = pl.pallas_call(
        functools.partial(_rw_scan_kernel, nb=nb, tc=tc), grid=(b // nb, l // tc),
        in_specs=[seq_spec] * 9 + [st_spec, _full((7, GROUP_W))],
        out_specs=[seq_spec, st_spec],
        out_shape=[jax.ShapeDtypeStruct((b, l, GROUP_W), F32), jax.ShapeDtypeStruct(s0p.shape, F32)],
        scratch_shapes=[pltpu.VMEM((nb, 2, HEAD_DIM, LANES), F32), pltpu.VMEM((nb, tc, GROUP_W), F32)],
        compiler_params=_cparams(("arbitrary", "arbitrary")))(*seqs, s0p, lw['rw_vec'])
    sfin = sfin.reshape(b, 2, HEAD_DIM, 2, HEAD_DIM).transpose(0, 1, 3, 2, 4).reshape(b, N_HEADS, HEAD_DIM, HEAD_DIM)
    return out.reshape(b * l, GROUP_W), sfin


HALO = 32


def _conv_kernel(u_ref, buf_ref, dw_ref, vec_ref, pw_ref, y_ref, st_ref, hp_ref, *, tl):
    j = pl.program_id(1)

    @pl.when(j == 0)
    def _():
        hp_ref[0:HALO - (CONV_W - 1), :] = jnp.zeros((HALO - (CONV_W - 1), GROUP_W), F32)
        hp_ref[HALO - (CONV_W - 1):HALO, :] = buf_ref[...]

    @pl.when(j > 0)
    def _():
        hp_ref[0:HALO, :] = hp_ref[tl:tl + HALO, :]

    u = u_ref[...]
    hp_ref[HALO:HALO + tl, :] = u[:, 0:GROUP_W] * jax.nn.sigmoid(u[:, GROUP_W:])
    acc = jnp.zeros((tl, GROUP_W), F32)
    for i in range(CONV_W):
        o = HALO - (CONV_W - 1) + i
        acc = acc + hp_ref[o:o + tl, :] * dw_ref[i:i + 1, :]
    y = acc + vec_ref[0:1, :]
    d = y - jnp.mean(y, axis=-1, keepdims=True)
    y = d * lax.rsqrt(jnp.mean(d * d, axis=-1, keepdims=True) + CONV_LN_EPS) * vec_ref[1:2, :] + vec_ref[2:3, :]
    y_ref[...] = _bdot(y * jax.nn.sigmoid(y), pw_ref[...])

    @pl.when(j == pl.num_programs(1) - 1)
    def _():
        st_ref[...] = hp_ref[tl + HALO - (CONV_W - 1):tl + HALO, :]


def _conv_mix(u, buf, lw, tl):
    b = buf.shape[0]
    l = u.shape[0] // b
    assert l % tl == 0
    nt = l // tl
    return pl.pallas_call(
        functools.partial(_conv_kernel, tl=tl), grid=(b, nt),
        in_specs=[pl.BlockSpec((tl, 2 * GROUP_W), lambda bi, j: (bi * nt + j, 0)),
                  pl.BlockSpec((None, CONV_W - 1, GROUP_W), lambda bi, j: (bi, 0, 0)),
                  _full((CONV_W, GROUP_W)), _full((3, GROUP_W)), _full((GROUP_W, GROUP_W))],
        out_specs=[pl.BlockSpec((tl, GROUP_W), lambda bi, j: (bi * nt + j, 0)),
                   pl.BlockSpec((None, CONV_W - 1, GROUP_W), lambda bi, j: (bi, 0, 0))],
        out_shape=[jax.ShapeDtypeStruct((b * l, GROUP_W), F32), jax.ShapeDtypeStruct(buf.shape, F32)],
        scratch_shapes=[pltpu.VMEM((tl + HALO, GROUP_W), F32)],
        compiler_params=_cparams(("arbitrary", "arbitrary")))(u, buf, lw['conv_dw'], lw['conv_vec'], lw['conv_pw'])


def _mla_prep_kernel(p_ref, cos_ref, sin_ref, qn_ref, kvn_ref, wqn_ref, wqr_ref, wqp_ref, wkb_ref,
                     rows_ref, keys_ref, q_ref):
    p = p_ref[...]
    cos, sin = cos_ref[...], sin_ref[...]
    c_q = _rms(p[:, 0:LANES], qn_ref[...]).astype(BF16)
    c_kv = _rms(p[:, LANES:2 * LANES], kvn_ref[...])
    k_rope = p[:, 2 * LANES:3 * LANES] * cos + p[:, 3 * LANES:4 * LANES] * sin
    rows_ref[:, 0:KV_LORA] = c_kv
    rows_ref[:, KV_LORA:C_MLA] = k_rope[:, 0:QK_ROPE]
    keys_ref[...] = jnp.concatenate([c_kv, k_rope], axis=-1).astype(BF16)
    for h in range(N_HEADS):
        q_nope = jnp.dot(c_q, wqn_ref[h], preferred_element_type=F32)
        q_lat = _bdot(q_nope, wkb_ref[h])
        q_rope = (jnp.dot(c_q, wqr_ref[h], preferred_element_type=F32) * cos
                  + jnp.dot(c_q, wqp_ref[h], preferred_element_type=F32) * sin)
        q_ref[h] = jnp.concatenate([q_lat, q_rope], axis=-1) * MLA_SCALE


def _mla_prep(p_mla, cos, sin, lw, tm):
    m = p_mla.shape[0]
    assert m % tm == 0
    row = lambda w: pl.BlockSpec((tm, w), lambda i: (i, 0))
    return pl.pallas_call(
        _mla_prep_kernel, grid=(m // tm,),
        in_specs=[row(MLA_W), row(LANES), row(LANES), _full((1, LANES)), _full((1, LANES)),
                  _full((N_HEADS, LANES, QK_NOPE)), _full((N_HEADS, LANES, LANES)), _full((N_HEADS, LANES, LANES)),
                  _full((N_HEADS, QK_NOPE, KV_LORA))],
        out_specs=[row(C_MLA), row(2 * LANES), pl.BlockSpec((N_HEADS, tm, 2 * LANES), lambda i: (0, i, 0))],
        out_shape=[jax.ShapeDtypeStruct((m, C_MLA), F32), jax.ShapeDtypeStruct((m, 2 * LANES), BF16),
                   jax.ShapeDtypeStruct((N_HEADS, m, 2 * LANES), F32)],
        compiler_params=_cparams(("parallel",)))(
            p_mla, cos, sin, lw['mla_q_norm'], lw['mla_kv_norm'], lw['mla_wq_nope'], lw['mla_wq_rope'],
            lw['mla_wq_rope_rot'], lw['mla_wkb'])


def _mla_out(o_lat, wvb_ref, rows_per_head):
    out = None
    for h in range(N_HEADS):
        t = _bdot(o_lat[h * rows_per_head:(h + 1) * rows_per_head], wvb_ref[h])
        out = t if out is None else out + t
    return out


def _mla_prompt_kernel(q_ref, keys_ref, wvb_ref, o_ref, *, tq):
    i = pl.program_id(1)
    q = q_ref[...].reshape(N_HEADS * tq, 2 * LANES).astype(BF16)
    t = i * tq + lax.broadcasted_iota(jnp.int32, (N_HEADS * tq, tq), 0) % tq
    col = lax.broadcasted_iota(jnp.int32, (N_HEADS * tq, tq), 1)

    def body(kt, carry):
        m, l, acc = carry
        k = keys_ref[pl.ds(pl.multiple_of(kt * tq, tq), tq), :]
        mask = kt * tq + col <= t
        s = jnp.where(mask, _bdot_t(q, k), NEG)
        m_new = jnp.maximum(m, jnp.max(s, axis=-1, keepdims=True))
        e = jnp.where(mask, jnp.exp(s - m_new), 0.0)
        alpha = jnp.exp(m - m_new)
        return m_new, alpha * l + jnp.sum(e, axis=-1, keepdims=True), alpha * acc + _bdot(e, k[:, 0:KV_LORA])

    init = (jnp.full((N_HEADS * tq, 1), NEG, F32), jnp.zeros((N_HEADS * tq, 1), F32),
            jnp.zeros((N_HEADS * tq, KV_LORA), F32))
    m, l, acc = lax.fori_loop(0, i + 1, body, init)
    o_ref[...] = _mla_out(acc / l, wvb_ref, tq)


def _mla_prompt(q, keys, wvb, b, tq):
    m = keys.shape[0]
    l = m // b
    assert l % tq == 0
    nq = l // tq
    return pl.pallas_call(
        functools.partial(_mla_prompt_kernel, tq=tq), grid=(b, nq),
        in_specs=[pl.BlockSpec((N_HEADS, tq, 2 * LANES), lambda bi, i: (0, bi * nq + i, 0)),
                  pl.BlockSpec((l, 2 * LANES), lambda bi, i: (bi, 0)), _full(wvb.shape)],
        out_specs=pl.BlockSpec((tq, GROUP_W), lambda bi, i: (bi * nq + i, 0)),
        out_shape=jax.ShapeDtypeStruct((m, GROUP_W), F32),
        compiler_params=_cparams(("parallel", "parallel")))(q, keys, wvb)


def _page_copies(table_ref, pool_ref, layer, buf_ref, sem_ref, b, slot, n_pages):
    return [pltpu.make_async_copy(pool_ref.at[layer, table_ref[b, j]], buf_ref.at[slot, j], sem_ref.at[slot])
            for j in range(n_pages)]


def _paged_fetch(table_ref, pool_ref, layer, buf_ref, sem_ref, n_pages):
    b = pl.program_id(0)
    slot = b % 2

    @pl.when(b == 0)
    def _():
        for c in _page_copies(table_ref, pool_ref, layer, buf_ref, sem_ref, b, slot, n_pages):
            c.start()

    @pl.when(b + 1 < pl.num_programs(0))
    def _():
        for c in _page_copies(table_ref, pool_ref, layer, buf_ref, sem_ref, b + 1, 1 - slot, n_pages):
            c.start()

    for c in _page_copies(table_ref, pool_ref, layer, buf_ref, sem_ref, b, slot, n_pages):
        c.wait()
    return slot


def _mla_sample_kernel(table_ref, q_ref, new_ref, pool_ref, wvb_ref, o_ref, buf_ref, sem_ref, *, layer, n_pages, lq):
    slot = _paged_fetch(table_ref, pool_ref, layer, buf_ref, sem_ref, n_pages)
    page = buf_ref.shape[2]
    past = buf_ref[slot].reshape(n_pages * page, C_MLA).astype(BF16)
    new = new_ref[...].astype(BF16)
    rows = N_HEADS * lq
    q = q_ref[...].reshape(rows, 2 * LANES)[:, 0:C_MLA].astype(BF16)
    s_past = _bdot_t(q, past)
    qi = lax.broadcasted_iota(jnp.int32, (rows, lq), 0) % lq
    mask = lax.broadcasted_iota(jnp.int32, (rows, lq), 1) <= qi
    s_new = jnp.where(mask, _bdot_t(q, new), NEG)
    m = jnp.maximum(jnp.max(s_past, axis=-1, keepdims=True), jnp.max(s_new, axis=-1, keepdims=True))
    e_past = jnp.exp(s_past - m)
    e_new = jnp.where(mask, jnp.exp(s_new - m), 0.0)
    den = jnp.sum(e_past, axis=-1, keepdims=True) + jnp.sum(e_new, axis=-1, keepdims=True)
    o_lat = (_bdot(e_past, past[:, 0:KV_LORA]) + _bdot(e_new, new[:, 0:KV_LORA])) / den
    o_ref[...] = _mla_out(o_lat, wvb_ref, lq)


def _mla_sample(q, new_rows, pool, page_table, wvb, layer, row0):
    b, n_pages = page_table.shape
    page = pool.shape[2]
    lq = (q.shape[1] - row0) // b
    assert row0 % lq == 0
    blk0 = row0 // lq
    grid_spec = pltpu.PrefetchScalarGridSpec(
        num_scalar_prefetch=1, grid=(b,),
        in_specs=[pl.BlockSpec((N_HEADS, lq, 2 * LANES), lambda i, tbl: (0, blk0 + i, 0)),
                  pl.BlockSpec((lq, C_MLA), lambda i, tbl: (blk0 + i, 0)),
                  pl.BlockSpec(memory_space=pl.ANY),
                  pl.BlockSpec(wvb.shape, lambda i, tbl: (0, 0, 0))],
        out_specs=pl.BlockSpec((lq, GROUP_W), lambda i, tbl: (i, 0)),
        scratch_shapes=[pltpu.VMEM((2, n_pages, page, C_MLA), F32), pltpu.SemaphoreType.DMA((2,))])
    return pl.pallas_call(
        functools.partial(_mla_sample_kernel, layer=layer, n_pages=n_pages, lq=lq), grid_spec=grid_spec,
        out_shape=jax.ShapeDtypeStruct((b * lq, GROUP_W), F32),
        compiler_params=_cparams(("arbitrary",)))(page_table, q, new_rows, pool, wvb)


def _gelu_tanh(x):
    return 0.5 * x * (1.0 + jnp.tanh(0.7978845608028654 * (x + 0.044715 * x * x * x)))


def _compress(ch, wexp_ref, pos_ref, b1_ref, w2k_ref, w2v_ref, b2_ref):
    n_ch = ch.shape[0]
    a = jnp.dot(ch, wexp_ref[...], preferred_element_type=F32)
    c = jnp.dot(pos_ref[...], wexp_ref[...], preferred_element_type=F32)
    up = lambda t: pltpu.roll(t, n_ch - 1, axis=0)
    hk = a[:, 0:LANES] + up(a[:, LANES:2 * LANES]) + c[0:1, 0:LANES] + c[1:2, LANES:2 * LANES] + b1_ref[:, 0:LANES]
    hv = (a[:, 2 * LANES:3 * LANES] + up(a[:, 3 * LANES:4 * LANES]) + c[0:1, 2 * LANES:3 * LANES]
          + c[1:2, 3 * LANES:4 * LANES] + b1_ref[:, LANES:2 * LANES])
    out = _bdot(_gelu_tanh(hk), w2k_ref[...]) + _bdot(_gelu_tanh(hv), w2v_ref[...]) + b2_ref[...]
    valid = lax.broadcasted_iota(jnp.int32, out.shape, 0) < n_ch - 1
    return jnp.where(valid, out, 0.0)


def _compress_kernel(ch_ref, wexp_ref, pos_ref, b1_ref, w2k_ref, w2v_ref, b2_ref, o_ref):
    o_ref[...] = _compress(ch_ref[...].astype(BF16), wexp_ref, pos_ref, b1_ref, w2k_ref, w2v_ref, b2_ref)


def _stack_heads(q, scale):
    low = lax.broadcasted_iota(jnp.int32, (q.shape[0], LANES), 1) < HEAD_DIM
    parts = []
    for h in range(N_HEADS):
        slab = q[:, (h // 2) * LANES:(h // 2 + 1) * LANES]
        if h % 2:
            slab = pltpu.roll(slab, HEAD_DIM, axis=1)
        parts.append(jnp.where(low, slab * scale, 0.0))
    return jnp.concatenate(parts, axis=0).astype(BF16)


def _softmax_parts(parts):
    m = None
    for s, mask in parts:
        mx = jnp.max(jnp.where(mask, s, NEG), axis=-1, keepdims=True)
        m = mx if m is None else jnp.maximum(m, mx)
    es = [jnp.where(mask, jnp.exp(jnp.where(mask, s, NEG) - m), 0.0) for s, mask in parts]
    den = sum(jnp.sum(e, axis=-1, keepdims=True) for e in es)
    return es, jnp.where(den > 0.0, den, 1.0)


def _select_blocks(p_c, t, n_cb, n_sb, nq):
    ncbp = p_c.shape[1]
    nsbp = -(-n_sb // LANES) * LANES
    psum = p_c[0:nq] + p_c[nq:2 * nq] + p_c[2 * nq:3 * nq] + p_c[3 * nq:4 * nq]
    ci = lax.broadcasted_iota(jnp.int32, (ncbp, nsbp), 0)
    sj = lax.broadcasted_iota(jnp.int32, (ncbp, nsbp), 1)
    overlap = ((CMP_STRIDE * ci < SLC_BLOCK * (sj + 1)) & (CMP_STRIDE * ci + CMP_LEN > SLC_BLOCK * sj)
               & (ci < n_cb) & (sj < n_sb)).astype(BF16)
    hi = psum.astype(BF16)
    lo = (psum - hi.astype(F32)).astype(BF16)
    imp = jnp.dot(hi, overlap, preferred_element_type=F32) + jnp.dot(lo, overlap, preferred_element_type=F32)
    j = lax.broadcasted_iota(jnp.int32, (nq, nsbp), 1)
    cur = t // SLC_BLOCK
    forced = (j == 0) | (j == cur) | (j == cur - 1)
    score = jnp.where(forced, FORCE_SCORE, jnp.where(j <= cur, imp, -1.0))
    score = jnp.where(j < n_sb, score, -3e38)
    sel = jnp.zeros((nq, nsbp), F32)
    for _ in range(min(N_SELECT, n_sb)):
        mx = jnp.max(score, axis=-1, keepdims=True)
        first = jnp.min(jnp.where(score == mx, j, nsbp), axis=-1, keepdims=True)
        hit = j == first
        sel = jnp.where(hit, 1.0, sel)
        score = jnp.where(hit, -3e38, score)
    return sel.astype(BF16)


def _tile4(x):
    return jnp.concatenate([x] * N_HEADS, axis=0)


def _gate_combine(gates, o_c, o_s, o_w, o_ref, nq):
    for h in range(N_HEADS):
        rows = slice(h * nq, (h + 1) * nq)
        g = lambda c: gates[:, 3 * h + c:3 * h + c + 1]
        o = g(0) * o_c[rows] + g(1) * o_s[rows] + g(2) * o_w[rows]
        o_ref[:, h * HEAD_DIM:(h + 1) * HEAD_DIM] = o[:, HEAD_DIM:LANES]


def _nsa_prompt_kernel(q_ref, g_ref, cmp_ref, slc_ref, win_ref, o_ref, slc_s, win_s, *, tq, l, tks):
    i = pl.program_id(1)

    @pl.when(i == 0)
    def _():
        slc_s[...] = slc_ref[...].astype(BF16)
        win_s[...] = win_ref[...].astype(BF16)

    rows = N_HEADS * tq
    n_cb, n_sb = l // CMP_STRIDE - 1, l // SLC_BLOCK
    q = _stack_heads(q_ref[...], HEAD_DIM ** -0.5)
    tq1 = i * tq + lax.broadcasted_iota(jnp.int32, (tq, 1), 0)
    t = _tile4(tq1)
    kcvc = cmp_ref[...].astype(BF16)
    n = lax.broadcasted_iota(jnp.int32, (rows, kcvc.shape[0]), 1)
    (e_c,), den = _softmax_parts([(_bdot_t(q, kcvc), (CMP_STRIDE * n + CMP_LEN <= t + 1) & (n < n_cb))])
    p_c = e_c / den
    o_c = _bdot(p_c, kcvc)
    sel = _select_blocks(p_c, tq1, n_cb, n_sb, tq)
    blk = lax.broadcasted_iota(jnp.int32, (sel.shape[1], tks), 0)
    kcol = lax.broadcasted_iota(jnp.int32, (sel.shape[1], tks), 1)
    col = lax.broadcasted_iota(jnp.int32, (rows, tks), 1)

    def body(kt, carry):
        m, lsum, acc = carry
        k0 = pl.multiple_of(kt * tks, tks)
        kv = slc_s[pl.ds(k0, tks), :]
        expand = (blk == (k0 + kcol) // SLC_BLOCK).astype(BF16)
        chosen = _tile4(jnp.dot(sel, expand, preferred_element_type=F32)) > 0.5
        mask = chosen & (k0 + col <= t)
        s = jnp.where(mask, _bdot_t(q, kv), NEG)
        m_new = jnp.maximum(m, jnp.max(s, axis=-1, keepdims=True))
        e = jnp.where(mask, jnp.exp(s - m_new), 0.0)
        alpha = jnp.exp(m - m_new)
        return m_new, alpha * lsum + jnp.sum(e, axis=-1, keepdims=True), alpha * acc + _bdot(e, kv)

    init = (jnp.full((rows, 1), NEG, F32), jnp.zeros((rows, 1), F32), jnp.zeros((rows, LANES), F32))
    _, lsum, acc = lax.fori_loop(0, ((i + 1) * tq + tks - 1) // tks, body, init)
    o_s = acc / jnp.where(lsum > 0.0, lsum, 1.0)
    span = WINDOW + tq
    w0 = pl.multiple_of(jnp.maximum(i * tq - WINDOW, 0), tq)
    kvw = win_s[pl.ds(w0, span), :]
    wpos = w0 + lax.broadcasted_iota(jnp.int32, (rows, span), 1)
    (e_w,), den = _softmax_parts([(_bdot_t(q, kvw), (wpos <= t) & (wpos > t - WINDOW))])
    o_w = _bdot(e_w, kvw) / den
    _gate_combine(jax.nn.sigmoid(g_ref[...]), o_c, o_s, o_w, o_ref, tq)


def _nsa_prompt(p_nsa, lw, b, tq):
    m = p_nsa.shape[0]
    l = m // b
    n_ch = l // CMP_STRIDE
    tks = 4 * SLC_BLOCK
    assert l % tq == 0 and l >= WINDOW + tq and l % tks == 0 and tq % SLC_BLOCK == 0
    ch = p_nsa[:, GROUP_W:GROUP_W + LANES].reshape(m // CMP_STRIDE, CMP_STRIDE * LANES)
    kcvc = pl.pallas_call(
        _compress_kernel, grid=(b,),
        in_specs=[pl.BlockSpec((n_ch, CMP_STRIDE * LANES), lambda bi: (bi, 0)), _full(lw['cmp_wexp_p'].shape),
                  _full(lw['cmp_pos_p'].shape), _full((1, 2 * LANES)), _full((LANES, LANES)), _full((LANES, LANES)),
                  _full((1, LANES))],
        out_specs=pl.BlockSpec((n_ch, LANES), lambda bi: (bi, 0)),
        out_shape=jax.ShapeDtypeStruct((b * n_ch, LANES), F32),
        compiler_params=_cparams(("parallel",)))(
            ch, lw['cmp_wexp_p'], lw['cmp_pos_p'], lw['cmp_b1'], lw['cmp_w2k'], lw['cmp_w2v'], lw['cmp_b2'])
    nq = l // tq
    slab = lambda c: pl.BlockSpec((l, LANES), lambda bi, i: (bi, c))
    return pl.pallas_call(
        functools.partial(_nsa_prompt_kernel, tq=tq, l=l, tks=tks), grid=(b, nq),
        in_specs=[pl.BlockSpec((tq, GROUP_W), lambda bi, i: (bi * nq + i, 0)),
                  pl.BlockSpec((tq, LANES), lambda bi, i: (bi * nq + i, NSA_W // LANES - 1)),
                  pl.BlockSpec((n_ch, LANES), lambda bi, i: (bi, 0)), slab(3), slab(4)],
        out_specs=pl.BlockSpec((tq, GROUP_W), lambda bi, i: (bi * nq + i, 0)),
        out_shape=jax.ShapeDtypeStruct((m, GROUP_W), F32),
        scratch_shapes=[pltpu.VMEM((l, LANES), BF16), pltpu.VMEM((l, LANES), BF16)],
        compiler_params=_cparams(("parallel", "arbitrary")))(p_nsa, p_nsa, kcvc, p_nsa, p_nsa)


def _nsa_sample_kernel(table_ref, q_ref, g_ref, nslc_ref, nwin_ref, win_ref, pool_ref, wexp_ref, pos_ref, b1_ref,
                       w2k_ref, w2v_ref, b2_ref, o_ref, wout_ref, buf_ref, sem_ref, *, layer, n_pages, lq, past):
    slot = _paged_fetch(table_ref, pool_ref, layer, buf_ref, sem_ref, n_pages)
    cpp = buf_ref.shape[2]
    n_ch = n_pages * cpp
    ch = buf_ref[slot].reshape(n_ch, buf_ref.shape[3]).astype(BF16)
    rows = N_HEADS * lq
    n_cb = (past + lq) // CMP_STRIDE - 1
    n_sb = -(-(past + lq) // SLC_BLOCK)
    q = _stack_heads(q_ref[...], HEAD_DIM ** -0.5)
    tq1 = past + lax.broadcasted_iota(jnp.int32, (lq, 1), 0)
    t = _tile4(tq1)
    kcvc = _compress(ch, wexp_ref, pos_ref, b1_ref, w2k_ref, w2v_ref, b2_ref).astype(BF16)
    n = lax.broadcasted_iota(jnp.int32, (rows, n_ch), 1)
    (e_c,), den = _softmax_parts([(_bdot_t(q, kcvc), (CMP_STRIDE * n + CMP_LEN <= t + 1) & (n < n_cb))])
    p_c = e_c / den
    o_c = _bdot(p_c, kcvc)
    sel = _select_blocks(p_c, tq1, n_cb, n_sb, lq)
    nsbp = sel.shape[1]
    blk = lax.broadcasted_iota(jnp.int32, (nsbp, n_ch), 0)
    ccol = lax.broadcasted_iota(jnp.int32, (nsbp, n_ch), 1)
    chosen = _tile4(jnp.dot(sel, (blk == ccol // (SLC_BLOCK // CMP_STRIDE)).astype(BF16),
                            preferred_element_type=F32)) > 0.5
    c_in = buf_ref.shape[3] // CMP_STRIDE
    slabs = [ch[:, r * c_in + LANES:r * c_in + 2 * LANES] for r in range(CMP_STRIDE)]
    parts = [(_bdot_t(q, kv), chosen) for kv in slabs]
    nslc = nslc_ref[...].astype(BF16)
    qi = lax.broadcasted_iota(jnp.int32, (rows, lq), 0) % lq
    causal_new = lax.broadcasted_iota(jnp.int32, (rows, lq), 1) <= qi
    new_blk = past // SLC_BLOCK
    sel_new = _tile4(sel[:, new_blk:new_blk + 1].astype(F32)) > 0.5
    parts.append((_bdot_t(q, nslc), causal_new & sel_new))
    es, den = _softmax_parts(parts)
    acc = _bdot(es[-1], nslc)
    for e, kv in zip(es[:-1], slabs):
        acc = acc + _bdot(e, kv)
    o_s = acc / den
    win = win_ref[...]
    nwin = nwin_ref[...]
    wb = win.shape[0]
    wpos = past - wb + lax.broadcasted_iota(jnp.int32, (rows, wb), 1)
    es, den = _softmax_parts([(_bdot_t(q, win), (wpos <= t) & (wpos > t - WINDOW) & (wpos >= 0)),
                              (_bdot_t(q, nwin), causal_new)])
    o_w = (_bdot(es[0], win) + _bdot(es[1], nwin)) / den
    _gate_combine(jax.nn.sigmoid(g_ref[...]), o_c, o_s, o_w, o_ref, lq)
    wout_ref[0:wb - lq, :] = win[lq:wb, :]
    wout_ref[wb - lq:wb, :] = nwin


def _nsa_sample(p_nsa, row0, pool, page_table, win_cache, lw, layer):
    b, n_pages = page_table.shape
    depth, n_pool, page, c_in = pool.shape
    lq = (p_nsa.shape[0] - row0) // b
    wb = win_cache.shape[1]
    assert row0 % lq == 0 and page % CMP_STRIDE == 0 and lq < CMP_STRIDE and lq % 8 == 0
    blk0 = row0 // lq
    cpp = page // CMP_STRIDE
    pool_ch = pool.reshape(depth, n_pool, cpp, CMP_STRIDE * c_in)
    row = lambda w, c: pl.BlockSpec((lq, w), lambda i, tbl: (blk0 + i, c))
    const = lambda a: pl.BlockSpec(a.shape, lambda i, tbl: (0,) * a.ndim)
    weights = [lw['cmp_wexp_s'], lw['cmp_pos_s'], lw['cmp_b1'], lw['cmp_w2k'], lw['cmp_w2v'], lw['cmp_b2']]
    grid_spec = pltpu.PrefetchScalarGridSpec(
        num_scalar_prefetch=1, grid=(b,),
        in_specs=[row(GROUP_W, 0), row(LANES, NSA_W // LANES - 1), row(LANES, 3), row(LANES, 4),
                  pl.BlockSpec((None, wb, LANES), lambda i, tbl: (i, 0, 0)),
                  pl.BlockSpec(memory_space=pl.ANY)] + [const(w) for w in weights],
        out_specs=[pl.BlockSpec((lq, GROUP_W), lambda i, tbl: (i, 0)),
                   pl.BlockSpec((None, wb, LANES), lambda i, tbl: (i, 0, 0))],
        scratch_shapes=[pltpu.VMEM((2, n_pages, cpp, CMP_STRIDE * c_in), F32), pltpu.SemaphoreType.DMA((2,))])
    return pl.pallas_call(
        functools.partial(_nsa_sample_kernel, layer=layer, n_pages=n_pages, lq=lq, past=n_pages * page),
        grid_spec=grid_spec,
        out_shape=[jax.ShapeDtypeStruct((b * lq, GROUP_W), F32), jax.ShapeDtypeStruct(win_cache.shape, F32)],
        compiler_params=_cparams(("arbitrary",)))(page_table, p_nsa, p_nsa, p_nsa, p_nsa, win_cache, pool_ch, *weights)


def _pad_cols(w, width):
    return jnp.pad(w, ((0, 0), (0, width - w.shape[1])))


def _rot_cols(w):
    half = QK_ROPE // 2
    return jnp.concatenate([-w[:, half:], w[:, :half]], axis=1)


def _expand_cmp_w1(w1, channels):
    half = CMP_STRIDE * HEAD_DIM
    out = jnp.zeros((CMP_STRIDE, channels, 4 * CMP_HID), F32)
    for br in range(2):
        for part in range(2):
            blk = w1[br, part * half:(part + 1) * half].reshape(CMP_STRIDE, HEAD_DIM, CMP_HID)
            c0 = (2 * br + part) * CMP_HID
            out = out.at[:, br * HEAD_DIM:(br + 1) * HEAD_DIM, c0:c0 + CMP_HID].set(blk)
    return out.reshape(CMP_STRIDE * channels, 4 * CMP_HID).astype(BF16)


def _expand_cmp_pos(pos, channels):
    out = jnp.zeros((8, CMP_STRIDE, channels), F32)
    for br in range(2):
        for part in range(2):
            out = out.at[part, :, br * HEAD_DIM:(br + 1) * HEAD_DIM].set(
                pos[br, part * CMP_STRIDE:(part + 1) * CMP_STRIDE])
    return out.reshape(8, CMP_STRIDE * channels).astype(BF16)


def _layer_weights(w, l, c_cache):
    nsa0 = RW_COLS
    mla0 = nsa0 + GROUP_W + 6 * HEAD_DIM + 3 * N_HEADS
    conv0 = mla0 + 2 * LANES + QK_ROPE
    w_in = w['w_in'][l]
    k_rope = w_in[:, mla0 + 2 * LANES:conv0]
    w_in_ext = jnp.concatenate([
        w_in[:, :nsa0], _pad_cols(w_in[:, nsa0:mla0], NSA_W), w_in[:, mla0:mla0 + 2 * LANES],
        _pad_cols(k_rope, LANES), _pad_cols(_rot_cols(k_rope), LANES), w_in[:, conv0:]], axis=1).astype(BF16)
    d_hd = QK_NOPE + QK_ROPE
    wqb = w['mla_w_qb'][l].reshape(-1, N_HEADS, d_hd).transpose(1, 0, 2)
    wq_rope = wqb[:, :, QK_NOPE:]
    pad3 = lambda a: jnp.pad(a, ((0, 0), (0, 0), (0, LANES - a.shape[2]))).astype(BF16)
    wvb = jnp.zeros((N_HEADS, KV_LORA, GROUP_W), F32)
    for h in range(N_HEADS):
        wvb = wvb.at[h, :, h * HEAD_DIM:(h + 1) * HEAD_DIM].set(w['mla_w_vb'][l][:, h, :])
    w2 = w['cmp_w2'][l]
    return dict(
        norms=w['norms'][l][:, None, :],
        ffn_wg=w['ffn_w_gate'][l].astype(BF16), ffn_wu=w['ffn_w_up'][l].astype(BF16),
        ffn_wd=w['ffn_w_down'][l].astype(BF16),
        w_in_ext=w_in_ext, w_out=w['w_out'][l].astype(BF16),
        rw_mu=w['rw_mu'][l][None, :], rw_vec=w['rw_vec'][l], rw_w2=w['rw_w2'][l].astype(BF16),
        rw_a2=w['rw_a2'][l].astype(BF16), rw_g2=w['rw_g2'][l].astype(BF16),
        cmp_wexp_p=_expand_cmp_w1(w['cmp_w1'][l], LANES), cmp_pos_p=_expand_cmp_pos(w['cmp_pos'][l], LANES),
        cmp_wexp_s=_expand_cmp_w1(w['cmp_w1'][l], c_cache), cmp_pos_s=_expand_cmp_pos(w['cmp_pos'][l], c_cache),
        cmp_b1=w['cmp_b1'][l].reshape(1, 2 * CMP_HID),
        cmp_w2k=_pad_cols(w2[0], LANES).astype(BF16),
        cmp_w2v=jnp.pad(w2[1], ((0, 0), (HEAD_DIM, 0))).astype(BF16),
        cmp_b2=w['cmp_b2'][l].reshape(1, 2 * HEAD_DIM),
        mla_q_norm=w['mla_q_norm'][l][None, :], mla_kv_norm=w['mla_kv_norm'][l][None, :],
        mla_wq_nope=wqb[:, :, :QK_NOPE].astype(BF16), mla_wq_rope=pad3(wq_rope),
        mla_wq_rope_rot=pad3(jnp.concatenate([-wq_rope[:, :, QK_ROPE // 2:], wq_rope[:, :, :QK_ROPE // 2]], axis=2)),
        mla_wkb=w['mla_w_kb'][l].transpose(1, 2, 0).astype(BF16), mla_wvb=wvb.astype(BF16),
        conv_dw=w['conv_dw'][l], conv_vec=w['conv_vec'][l], conv_pw=w['conv_pw'][l].astype(BF16),
        x_wq=w['x_wq'][l].astype(BF16), x_wkv=w['x_wkv'][l].astype(BF16), x_wo=w['x_wo'][l].astype(BF16))


def _rope_tables(pos):
    half = QK_ROPE // 2
    inv = ROPE_THETA ** (-jnp.arange(half, dtype=F32) / half)
    ang = pos.astype(F32)[:, None] * inv
    tile = lambda a: jnp.tile(a, (1, LANES // half))
    return tile(jnp.cos(ang)), tile(jnp.sin(ang))


def _pick(n, *cands):
    for c in cands:
        if n % c == 0:
            return c
    return n


def kernel(x_prompt, x_sample, cache_mla, cache_nsa, cache_nsa_win, cache_mem, state_rwkv, state_rwkv_shift,
           state_conv, page_table, mem_prompt, norms, ffn_w_gate, ffn_w_up, ffn_w_down, w_in, w_out, rw_mu, rw_vec,
           rw_w2, rw_a2, rw_g2, cmp_pos, cmp_w1, cmp_b1, cmp_w2, cmp_b2, mla_q_norm, mla_kv_norm, mla_w_qb, mla_w_kb,
           mla_w_vb, conv_dw, conv_vec, conv_pw, x_wq, x_wkv, x_wo, final_norm):
    w = dict(norms=norms, ffn_w_gate=ffn_w_gate, ffn_w_up=ffn_w_up, ffn_w_down=ffn_w_down, w_in=w_in, w_out=w_out,
             rw_mu=rw_mu, rw_vec=rw_vec, rw_w2=rw_w2, rw_a2=rw_a2, rw_g2=rw_g2, cmp_pos=cmp_pos, cmp_w1=cmp_w1,
             cmp_b1=cmp_b1, cmp_w2=cmp_w2, cmp_b2=cmp_b2, mla_q_norm=mla_q_norm, mla_kv_norm=mla_kv_norm,
             mla_w_qb=mla_w_qb, mla_w_kb=mla_w_kb, mla_w_vb=mla_w_vb, conv_dw=conv_dw, conv_vec=conv_vec,
             conv_pw=conv_pw, x_wq=x_wq, x_wkv=x_wkv, x_wo=x_wo)
    depth = norms.shape[0]
    bp, lp, d = x_prompt.shape
    bs, ls, _ = x_sample.shape
    mp, ms = bp * lp, bs * ls
    m = mp + ms
    n_pages, page = page_table.shape[1], cache_mla.shape[2]
    past = n_pages * page
    d_ff = ffn_w_gate.shape[-1]
    chunk = _ff_chunk(d_ff)
    tm = _pick(m, 512, 256, 128, 64, 8)
    n_mem = mem_prompt.shape[1]

    x = jnp.concatenate([x_prompt.reshape(mp, d), x_sample.reshape(ms, d)], axis=0)
    cos_p, sin_p = _rope_tables(jnp.arange(lp, dtype=jnp.int32))
    cos_s, sin_s = _rope_tables(past + jnp.arange(ls, dtype=jnp.int32))
    cos = jnp.concatenate([jnp.tile(cos_p, (bp, 1)), jnp.tile(cos_s, (bs, 1))], axis=0)
    sin = jnp.concatenate([jnp.tile(sin_p, (bp, 1)), jnp.tile(sin_s, (bs, 1))], axis=0)
    mem_rows = mem_prompt.reshape(bp * n_mem, d)
    zeros = lambda *s: jnp.zeros(s, F32)
    sp, ss = [], []
    both = lambda a, c: jnp.concatenate([a, c], axis=0)

    for l in range(depth):
        lw = _layer_weights(w, l, cache_nsa.shape[3])
        n = lw['norms']
        ffn_w = lambda i: (lw['ffn_wg'][i], lw['ffn_wu'][i])
        (h1,) = _token_call(functools.partial(_ffn_a_kernel, chunk=chunk), [x], [n[0], *ffn_w(0)], [d_ff], [BF16], tm)
        x1, p_rw, p_nsa, p_mla, p_conv = _token_call(
            _ffn_b_proj_kernel, [x, h1], [lw['ffn_wd'][0], n[1], lw['w_in_ext']],
            [d, RW_COLS, NSA_W, MLA_W, 2 * GROUP_W], [F32] * 5, tm)
        (mem_kv,) = _token_call(_norm_proj_kernel, [mem_rows], [n[3], lw['x_wkv']], [2 * GROUP_W], [F32],
                                _pick(bp * n_mem, 256, 8))
        mem_kv = mem_kv.reshape(bp, n_mem, 2 * GROUP_W)
        o_rw_p, rw_state_p = _rwkv_mix(p_rw[:mp], zeros(bp, RW_COLS), zeros(bp, N_HEADS, HEAD_DIM, HEAD_DIM), lw,
                                       nb=bp, tc=_pick(lp, 256, 8), tl=_pick(lp, 512, 8))
        o_rw_s, rw_state_s = _rwkv_mix(p_rw[mp:], state_rwkv_shift[l], state_rwkv[l], lw,
                                       nb=_pick(bs, 4, 1), tc=ls, tl=ls)
        o_nsa_p = _nsa_prompt(p_nsa[:mp], lw, bp, tq=_pick(lp, 128))
        o_nsa_s, win_s = _nsa_sample(p_nsa, mp, cache_nsa, page_table, cache_nsa_win[l], lw, l)
        mla_rows, mla_keys, mla_q = _mla_prep(p_mla, cos, sin, lw, tm)
        o_mla_p = _mla_prompt(mla_q[:, :mp], mla_keys[:mp], lw['mla_wvb'], bp, tq=_pick(lp, 256, 8))
        o_mla_s = _mla_sample(mla_q, mla_rows, cache_mla, page_table, lw['mla_wvb'], l, mp)
        o_conv_p, conv_p = _conv_mix(p_conv[:mp], zeros(bp, CONV_W - 1, GROUP_W), lw, tl=_pick(lp, 512, 8))
        o_conv_s, conv_s = _conv_mix(p_conv[mp:], state_conv[l], lw, tl=ls)
        x2, qx = _token_call(
            _mix_out_kernel,
            [x1, both(o_rw_p, o_rw_s), both(o_nsa_p, o_nsa_s), both(o_mla_p, o_mla_s), both(o_conv_p, o_conv_s)],
            [lw['w_out'], n[2], lw['x_wq']], [d, GROUP_W], [F32] * 2, tm)
        ox = both(_cross_attend(qx[:mp], mem_kv, 1, _pick(lp, 512, 8)),
                  _cross_attend(qx[mp:], cache_mem[l], _pick(bs, 8, 1), ls))
        x3, h2 = _token_call(functools.partial(_xo_ffn_a_kernel, chunk=chunk), [x2, ox],
                             [lw['x_wo'], n[4], *ffn_w(1)], [d, d_ff], [F32, BF16], tm)
        (x,) = _token_call(_ffn_b_kernel, [x3, h2], [lw['ffn_wd'][1]], [d], [F32], tm)

        nsa_rows = p_nsa[:, GROUP_W:GROUP_W + 4 * HEAD_DIM]
        keep = min(WINDOW, lp)
        nsa_win_p = p_nsa[:mp, GROUP_W + 4 * HEAD_DIM:GROUP_W + 6 * HEAD_DIM].reshape(bp, lp, 2 * HEAD_DIM)[:, lp - keep:]
        sp.append((mla_rows[:mp].reshape(bp, lp, C_MLA), nsa_rows[:mp].reshape(bp, lp, 4 * HEAD_DIM), nsa_win_p,
                   mem_kv, rw_state_p, p_rw[:mp].reshape(bp, lp, RW_COLS)[:, -1], conv_p))
        ss.append((mla_rows[mp:].reshape(bs, ls, C_MLA), nsa_rows[mp:].reshape(bs, ls, 4 * HEAD_DIM), win_s,
                   rw_state_s, p_rw[mp:].reshape(bs, ls, RW_COLS)[:, -1], conv_s))

    (y,) = _token_call(_norm_kernel, [x], [final_norm[None, :]], [d], [F32], tm)
    stack = lambda states, i: jnp.stack([s[i] for s in states])
    return (y[:mp].reshape(bp, lp, d), y[mp:].reshape(bs, ls, d),
            *(stack(sp, i) for i in range(7)), *(stack(ss, i) for i in range(6)))
```

```python
import functools

import jax
import jax.numpy as jnp
from jax import lax
from jax.experimental import pallas as pl
from jax.experimental.pallas import tpu as pltpu

F32 = jnp.float32
BF16 = jnp.bfloat16
HIGHEST = lax.Precision.HIGHEST

GROUP_W = 256
HEAD_DIM = 64
N_HEADS = 4
LANES = 128
SUBLANES = 8
W_LORA, A_LORA, G_LORA = 32, 32, 64
RW_COLS = 3 * GROUP_W + W_LORA + A_LORA + G_LORA
RW_GN_EPS = 64e-5
CMP_LEN, CMP_STRIDE, CMP_HID = 32, 16, 128
SLC_BLOCK, N_SELECT, WINDOW = 64, 16, 512
KV_LORA, QK_NOPE, QK_ROPE = 128, 64, 32
C_MLA = KV_LORA + QK_ROPE
MLA_SCALE = (QK_NOPE + QK_ROPE) ** -0.5
ROPE_THETA = 10000.0
CONV_W = 31
CONV_LN_EPS = 1e-5
NORM_EPS = 1e-6
NEG = -1e30
FORCE_SCORE = 1e4
NSA_W = 768
MLA_W = 512
VMEM_LIMIT = 56 * 1024 * 1024


def _cparams(sem, vmem=VMEM_LIMIT):
    return pltpu.CompilerParams(dimension_semantics=sem, vmem_limit_bytes=vmem)


def _rms(x, g):
    return x * lax.rsqrt(jnp.mean(x * x, axis=-1, keepdims=True) + NORM_EPS) * g


def _bdot(a, b):
    return jnp.dot(a.astype(BF16), b.astype(BF16), preferred_element_type=F32)


def _bdot_t(a, b):
    return lax.dot_general(a.astype(BF16), b.astype(BF16), (((1,), (1,)), ((), ())), preferred_element_type=F32)


def _seg_ones(n, seg):
    r = lax.broadcasted_iota(jnp.int32, (n, n), 0) // seg
    c = lax.broadcasted_iota(jnp.int32, (n, n), 1) // seg
    return (r == c).astype(F32)


def _full(shape):
    nd = len(shape)
    return pl.BlockSpec(shape, lambda *_: (0,) * nd)


def _swiglu_to(xn, wg_ref, wu_ref, h_ref, chunk):
    for c in range(h_ref.shape[1] // chunk):
        sl = slice(c * chunk, (c + 1) * chunk)
        gate = jnp.dot(xn, wg_ref[:, sl], preferred_element_type=F32)
        up = jnp.dot(xn, wu_ref[:, sl], preferred_element_type=F32)
        h_ref[:, sl] = (gate * jax.nn.sigmoid(gate) * up).astype(BF16)


def _ffn_a_kernel(x_ref, g_ref, wg_ref, wu_ref, h_ref, *, chunk):
    xn = _rms(x_ref[...], g_ref[...]).astype(BF16)
    _swiglu_to(xn, wg_ref, wu_ref, h_ref, chunk)


def _ffn_b_proj_kernel(x_ref, h_ref, wd_ref, g_ref, win_ref, x1_ref, prw_ref, pnsa_ref, pmla_ref, pconv_ref):
    x1 = x_ref[...] + 0.5 * jnp.dot(h_ref[...], wd_ref[...], preferred_element_type=F32)
    x1_ref[...] = x1
    xn = _rms(x1, g_ref[...]).astype(BF16)
    off = 0
    for ref in (prw_ref, pnsa_ref, pmla_ref, pconv_ref):
        w = ref.shape[1]
        ref[...] = jnp.dot(xn, win_ref[:, off:off + w], preferred_element_type=F32)
        off += w


def _mix_out_kernel(x_ref, orw_ref, onsa_ref, omla_ref, oconv_ref, wo_ref, g_ref, wq_ref, x2_ref, q_ref):
    acc = x_ref[...]
    for i, ref in enumerate((orw_ref, onsa_ref, omla_ref, oconv_ref)):
        acc = acc + _bdot(ref[...], wo_ref[i * GROUP_W:(i + 1) * GROUP_W, :])
    x2_ref[...] = acc
    q_ref[...] = _bdot(_rms(acc, g_ref[...]), wq_ref[...]) * (HEAD_DIM ** -0.5)


def _xo_ffn_a_kernel(x_ref, ox_ref, wo_ref, g_ref, wg_ref, wu_ref, x3_ref, h_ref, *, chunk):
    x3 = x_ref[...] + _bdot(ox_ref[...], wo_ref[...])
    x3_ref[...] = x3
    _swiglu_to(_rms(x3, g_ref[...]).astype(BF16), wg_ref, wu_ref, h_ref, chunk)


def _ffn_b_kernel(x_ref, h_ref, wd_ref, o_ref):
    o_ref[...] = x_ref[...] + 0.5 * jnp.dot(h_ref[...], wd_ref[...], preferred_element_type=F32)


def _norm_kernel(x_ref, g_ref, o_ref):
    o_ref[...] = _rms(x_ref[...], g_ref[...])


def _norm_proj_kernel(x_ref, g_ref, w_ref, o_ref):
    o_ref[...] = _bdot(_rms(x_ref[...], g_ref[...]), w_ref[...])


def _token_call(body, tiled_in, full_in, out_widths, out_dtypes, tm):
    m = tiled_in[0].shape[0]
    assert m % tm == 0
    in_specs = [pl.BlockSpec((tm, a.shape[1]), lambda i: (i, 0)) for a in tiled_in]
    in_specs += [_full(a.shape) for a in full_in]
    out_specs = [pl.BlockSpec((tm, w), lambda i: (i, 0)) for w in out_widths]
    out_shape = [jax.ShapeDtypeStruct((m, w), dt) for w, dt in zip(out_widths, out_dtypes)]
    return pl.pallas_call(body, grid=(m // tm,), in_specs=in_specs, out_specs=out_specs, out_shape=out_shape,
                          compiler_params=_cparams(("parallel",)))(*tiled_in, *full_in)


def _ff_chunk(d_ff):
    return 256 if d_ff % 256 == 0 else LANES


def _xattn_kernel(q_ref, kv_ref, o_ref, *, groups, lq):
    for g in range(groups):
        rows = slice(g * lq, (g + 1) * lq)
        for h in range(N_HEADS):
            cols = slice(h * HEAD_DIM, (h + 1) * HEAD_DIM)
            k = kv_ref[g, :, cols]
            v = kv_ref[g, :, GROUP_W + h * HEAD_DIM:GROUP_W + (h + 1) * HEAD_DIM]
            s = _bdot_t(q_ref[rows, cols], k)
            e = jnp.exp(s - jnp.max(s, axis=-1, keepdims=True))
            p = e / jnp.sum(e, axis=-1, keepdims=True)
            o_ref[rows, cols] = _bdot(p, v)


def _cross_attend(q, mem_kv, groups, lq_tile):
    b, n_mem, _ = mem_kv.shape
    l = q.shape[0] // b
    if groups > 1:
        assert lq_tile == l and b % groups == 0
        grid = (b // groups,)
        q_spec = pl.BlockSpec((groups * l, GROUP_W), lambda i: (i, 0))
        kv_spec = pl.BlockSpec((groups, n_mem, 2 * GROUP_W), lambda i: (i, 0, 0))
    else:
        assert l % lq_tile == 0
        nq = l // lq_tile
        grid = (b, nq)
        q_spec = pl.BlockSpec((lq_tile, GROUP_W), lambda bi, i: (bi * nq + i, 0))
        kv_spec = pl.BlockSpec((1, n_mem, 2 * GROUP_W), lambda bi, i: (bi, 0, 0))
    return pl.pallas_call(
        functools.partial(_xattn_kernel, groups=groups, lq=lq_tile), grid=grid, in_specs=[q_spec, kv_spec],
        out_specs=q_spec, out_shape=jax.ShapeDtypeStruct(q.shape, F32),
        compiler_params=_cparams(("parallel",) * len(grid)))(q, mem_kv)


def _rw_prep_kernel(p_ref, prev_ref, mu_ref, vec_ref, w2_ref, a2_ref, g2_ref,
                    rp_ref, nkk_ref, w_ref, kka_ref, k_ref, v_ref, ovk_ref, g_ref, bonus_ref, carry_ref):
    j = pl.program_id(1)

    @pl.when(j == 0)
    def _():
        carry_ref[...] = prev_ref[...]

    p = p_ref[...]
    tl = p.shape[0]
    row = lax.broadcasted_iota(jnp.int32, p.shape, 0)
    prev = jnp.where(row == 0, carry_ref[...], pltpu.roll(p, 1, axis=0))
    carry_ref[...] = p[tl - 1:tl, :]
    xs = p + (prev - p) * mu_ref[...]
    r, k, v = xs[:, 0:GROUP_W], xs[:, GROUP_W:2 * GROUP_W], xs[:, 2 * GROUP_W:3 * GROUP_W]
    o = 3 * GROUP_W
    xw, xa, xg = xs[:, o:o + W_LORA], xs[:, o + W_LORA:o + W_LORA + A_LORA], xs[:, o + W_LORA + A_LORA:]
    w0, a0, k_k, k_a, r_k = (vec_ref[i:i + 1, :] for i in range(5))
    z = -(w0 + _bdot(jnp.tanh(xw), w2_ref[...]))
    softplus = jnp.maximum(z, 0.0) + jnp.log1p(jnp.exp(-jnp.abs(z)))
    decay = jnp.exp(-jnp.exp(-softplus - 0.5))
    a = jax.nn.sigmoid(a0 + _bdot(xa, a2_ref[...]))
    g_ref[...] = _bdot(jax.nn.sigmoid(xg), g2_ref[...])
    ones = _seg_ones(GROUP_W, HEAD_DIM)
    seg = lambda t: jnp.dot(t, ones, precision=HIGHEST, preferred_element_type=F32)
    kk = k * k_k
    kk = kk / jnp.maximum(jnp.sqrt(seg(kk * kk)), 1e-12)
    k_eff = k * (1.0 + (a - 1.0) * k_a)
    nkk = -kk
    kka = kk * a
    rp_ref[...] = decay * r + nkk * seg(kka * r)
    ovk_ref[...] = v * seg(k_eff * r)
    bonus_ref[...] = seg(r * k_eff * r_k) * v
    nkk_ref[...] = nkk
    w_ref[...] = decay
    kka_ref[...] = kka
    k_ref[...] = k_eff
    v_ref[...] = v


def _rw_scan_kernel(rp_ref, nkk_ref, w_ref, kka_ref, k_ref, v_ref, ovk_ref, g_ref, bonus_ref, s0_ref, vec_ref,
                    out_ref, sfin_ref, s_ref, o_ref, *, nb, tc):
    j = pl.program_id(1)

    @pl.when(j == 0)
    def _():
        s_ref[...] = s0_ref[...]

    pairs = [(b, p) for b in range(nb) for p in range(N_HEADS // 2)]
    left1 = lax.broadcasted_iota(jnp.int32, (1, 1, LANES), 2) < HEAD_DIM
    left = lax.broadcasted_iota(jnp.int32, (1, HEAD_DIM, LANES), 2) < HEAD_DIM
    eye2 = (lax.broadcasted_iota(jnp.int32, (1, HEAD_DIM, LANES), 1)
            == lax.broadcasted_iota(jnp.int32, (1, HEAD_DIM, LANES), 2) % HEAD_DIM).astype(F32)
    rsum = lambda t: jnp.sum(t, axis=2, keepdims=True)

    def halves(rw):
        return jnp.where(left1, rw, 0.0), jnp.where(left1, 0.0, rw)

    def step(t8, carry):
        base = pl.multiple_of(t8 * SUBLANES, SUBLANES)
        ld = lambda ref: jnp.stack([ref[b, pl.ds(base, SUBLANES), p * LANES:(p + 1) * LANES] for b, p in pairs])
        nkk, rp, v, w, kka, k = (ld(r) for r in (nkk_ref, rp_ref, v_ref, w_ref, kka_ref, k_ref))
        s = s_ref[...].reshape(len(pairs), HEAD_DIM, LANES)
        o_rows = []
        for i in range(SUBLANES):
            row = lambda x: x[:, i:i + 1, :]
            nkk_l, nkk_r = halves(row(nkk))
            rp_l, rp_r = halves(row(rp))
            v_l, v_r = halves(row(v))
            sa = jnp.where(left, rsum(s * nkk_l), rsum(s * nkk_r))
            oc = jnp.where(left, rsum(s * rp_l), rsum(s * rp_r))
            vc = jnp.where(left, rsum(eye2 * v_l), rsum(eye2 * v_r))
            s = s * row(w) + sa * row(kka) + vc * row(k)
            o_rows.append(jnp.sum(oc * eye2, axis=1, keepdims=True))
        s_ref[...] = s.reshape(s_ref.shape)
        o_new = jnp.concatenate(o_rows, axis=1) + ld(ovk_ref)
        for n, (b, p) in enumerate(pairs):
            o_ref[b, pl.ds(base, SUBLANES), p * LANES:(p + 1) * LANES] = o_new[n]
        return carry

    lax.fori_loop(0, tc // SUBLANES, step, 0)

    @pl.when(j == pl.num_programs(1) - 1)
    def _():
        sfin_ref[...] = s_ref[...]

    mean = _seg_ones(GROUP_W, HEAD_DIM) * (1.0 / HEAD_DIM)
    seg_mean = lambda t: jnp.dot(t, mean, precision=HIGHEST, preferred_element_type=F32)
    ln_w, ln_b = vec_ref[5:6, :], vec_ref[6:7, :]
    for b in range(nb):
        o = o_ref[b]
        d = o - seg_mean(o)
        on = d * lax.rsqrt(seg_mean(d * d) + RW_GN_EPS) * ln_w + ln_b
        out_ref[b] = (on + bonus_ref[b]) * g_ref[b]


def _rwkv_mix(p, p_prev, s0, lw, nb, tc, tl):
    b = p_prev.shape[0]
    l = p.shape[0] // b
    assert l % tl == 0 and l % tc == 0 and b % nb == 0 and tc % SUBLANES == 0
    nt = l // tl
    tile = pl.BlockSpec((tl, GROUP_W), lambda bi, j: (bi * nt + j, 0))
    outs = pl.pallas_call(
        _rw_prep_kernel, grid=(b, nt),
        in_specs=[pl.BlockSpec((tl, RW_COLS), lambda bi, j: (bi * nt + j, 0)),
                  pl.BlockSpec((None, 1, RW_COLS), lambda bi, j: (bi, 0, 0)),
                  _full((1, RW_COLS)), _full((7, GROUP_W)), _full((W_LORA, GROUP_W)), _full((A_LORA, GROUP_W)),
                  _full((G_LORA, GROUP_W))],
        out_specs=[tile] * 9, out_shape=[jax.ShapeDtypeStruct((b * l, GROUP_W), F32)] * 9,
        scratch_shapes=[pltpu.VMEM((1, RW_COLS), F32)],
        compiler_params=_cparams(("arbitrary", "arbitrary")))(
            p, p_prev[:, None, :], lw['rw_mu'], lw['rw_vec'], lw['rw_w2'], lw['rw_a2'], lw['rw_g2'])
    seqs = [a.reshape(b, l, GROUP_W) for a in outs]
    s0p = s0.reshape(b, 2, 2, HEAD_DIM, HEAD_DIM).transpose(0, 1, 3, 2, 4).reshape(b, 2, HEAD_DIM, LANES)
    seq_spec = pl.BlockSpec((nb, tc, GROUP_W), lambda bi, j: (bi, j, 0))
    st_spec = pl.BlockSpec((nb, 2, HEAD_DIM, LANES), lambda bi, j: (bi, 0, 0, 0))
    out, sfin = pl.pallas_call(
        functools.partial(_rw_scan_kernel, nb=nb, tc=tc), grid=(b // nb, l // tc),
        in_specs=[seq_spec] * 9 + [st_spec, _full((7, GROUP_W))],
        out_specs=[seq_spec, st_spec],
        out_shape=[jax.ShapeDtypeStruct((b, l, GROUP_W), F32), jax.ShapeDtypeStruct(s0p.shape, F32)],
        scratch_shapes=[pltpu.VMEM((nb, 2, HEAD_DIM, LANES), F32), pltpu.VMEM((nb, tc, GROUP_W), F32)],
        compiler_params=_cparams(("arbitrary", "arbitrary")))(*seqs, s0p, lw['rw_vec'])
    sfin = sfin.reshape(b, 2, HEAD_DIM, 2, HEAD_DIM).transpose(0, 1, 3, 2, 4).reshape(b, N_HEADS, HEAD_DIM, HEAD_DIM)
    return out.reshape(b * l, GROUP_W), sfin


HALO = 32


def _conv_kernel(u_ref, buf_ref, dw_ref, vec_ref, pw_ref, y_ref, st_ref, hp_ref, *, tl):
    j = pl.program_id(1)

    @pl.when(j == 0)
    def _():
        hp_ref[0:HALO - (CONV_W - 1), :] = jnp.zeros((HALO - (CONV_W - 1), GROUP_W), F32)
        hp_ref[HALO - (CONV_W - 1):HALO, :] = buf_ref[...]

    @pl.when(j > 0)
    def _():
        hp_ref[0:HALO, :] = hp_ref[tl:tl + HALO, :]

    u = u_ref[...]
    hp_ref[HALO:HALO + tl, :] = u[:, 0:GROUP_W] * jax.nn.sigmoid(u[:, GROUP_W:])
    acc = jnp.zeros((tl, GROUP_W), F32)
    for i in range(CONV_W):
        o = HALO - (CONV_W - 1) + i
        acc = acc + hp_ref[o:o + tl, :] * dw_ref[i:i + 1, :]
    y = acc + vec_ref[0:1, :]
    d = y - jnp.mean(y, axis=-1, keepdims=True)
    y = d * lax.rsqrt(jnp.mean(d * d, axis=-1, keepdims=True) + CONV_LN_EPS) * vec_ref[1:2, :] + vec_ref[2:3, :]
    y_ref[...] = _bdot(y * jax.nn.sigmoid(y), pw_ref[...])

    @pl.when(j == pl.num_programs(1) - 1)
    def _():
        st_ref[...] = hp_ref[tl + HALO - (CONV_W - 1):tl + HALO, :]


def _conv_mix(u, buf, lw, tl):
    b = buf.shape[0]
    l = u.shape[0] // b
    assert l % tl == 0
    nt = l // tl
    return pl.pallas_call(
        functools.partial(_conv_kernel, tl=tl), grid=(b, nt),
        in_specs=[pl.BlockSpec((tl, 2 * GROUP_W), lambda bi, j: (bi * nt + j, 0)),
                  pl.BlockSpec((None, CONV_W - 1, GROUP_W), lambda bi, j: (bi, 0, 0)),
                  _full((CONV_W, GROUP_W)), _full((3, GROUP_W)), _full((GROUP_W, GROUP_W))],
        out_specs=[pl.BlockSpec((tl, GROUP_W), lambda bi, j: (bi * nt + j, 0)),
                   pl.BlockSpec((None, CONV_W - 1, GROUP_W), lambda bi, j: (bi, 0, 0))],
        out_shape=[jax.ShapeDtypeStruct((b * l, GROUP_W), F32), jax.ShapeDtypeStruct(buf.shape, F32)],
        scratch_shapes=[pltpu.VMEM((tl + HALO, GROUP_W), F32)],
        compiler_params=_cparams(("arbitrary", "arbitrary")))(u, buf, lw['conv_dw'], lw['conv_vec'], lw['conv_pw'])


def _mla_prep_kernel(p_ref, cos_ref, sin_ref, qn_ref, kvn_ref, wqn_ref, wqr_ref, wqp_ref, wkb_ref,
                     rows_ref, keys_ref, q_ref):
    p = p_ref[...]
    cos, sin = cos_ref[...], sin_ref[...]
    c_q = _rms(p[:, 0:LANES], qn_ref[...]).astype(BF16)
    c_kv = _rms(p[:, LANES:2 * LANES], kvn_ref[...])
    k_rope = p[:, 2 * LANES:3 * LANES] * cos + p[:, 3 * LANES:4 * LANES] * sin
    rows_ref[:, 0:KV_LORA] = c_kv
    rows_ref[:, KV_LORA:C_MLA] = k_rope[:, 0:QK_ROPE]
    keys_ref[...] = jnp.concatenate([c_kv, k_rope], axis=-1).astype(BF16)
    for h in range(N_HEADS):
        q_nope = jnp.dot(c_q, wqn_ref[h], preferred_element_type=F32)
        q_lat = _bdot(q_nope, wkb_ref[h])
        q_rope = (jnp.dot(c_q, wqr_ref[h], preferred_element_type=F32) * cos
                  + jnp.dot(c_q, wqp_ref[h], preferred_element_type=F32) * sin)
        q_ref[h] = jnp.concatenate([q_lat, q_rope], axis=-1) * MLA_SCALE


def _mla_prep(p_mla, cos, sin, lw, tm):
    m = p_mla.shape[0]
    assert m % tm == 0
    row = lambda w: pl.BlockSpec((tm, w), lambda i: (i, 0))
    return pl.pallas_call(
        _mla_prep_kernel, grid=(m // tm,),
        in_specs=[row(MLA_W), row(LANES), row(LANES), _full((1, LANES)), _full((1, LANES)),
                  _full((N_HEADS, LANES, QK_NOPE)), _full((N_HEADS, LANES, LANES)), _full((N_HEADS, LANES, LANES)),
                  _full((N_HEADS, QK_NOPE, KV_LORA))],
        out_specs=[row(C_MLA), row(2 * LANES), pl.BlockSpec((N_HEADS, tm, 2 * LANES), lambda i: (0, i, 0))],
        out_shape=[jax.ShapeDtypeStruct((m, C_MLA), F32), jax.ShapeDtypeStruct((m, 2 * LANES), BF16),
                   jax.ShapeDtypeStruct((N_HEADS, m, 2 * LANES), F32)],
        compiler_params=_cparams(("parallel",)))(
            p_mla, cos, sin, lw['mla_q_norm'], lw['mla_kv_norm'], lw['mla_wq_nope'], lw['mla_wq_rope'],
            lw['mla_wq_rope_rot'], lw['mla_wkb'])


def _mla_out(o_lat, wvb_ref, rows_per_head):
    out = None
    for h in range(N_HEADS):
        t = _bdot(o_lat[h * rows_per_head:(h + 1) * rows_per_head], wvb_ref[h])
        out = t if out is None else out + t
    return out


def _mla_prompt_kernel(q_ref, keys_ref, wvb_ref, o_ref, *, tq):
    i = pl.program_id(1)
    q = q_ref[...].reshape(N_HEADS * tq, 2 * LANES).astype(BF16)
    t = i * tq + lax.broadcasted_iota(jnp.int32, (N_HEADS * tq, tq), 0) % tq
    col = lax.broadcasted_iota(jnp.int32, (N_HEADS * tq, tq), 1)

    def body(kt, carry):
        m, l, acc = carry
        k = keys_ref[pl.ds(pl.multiple_of(kt * tq, tq), tq), :]
        mask = kt * tq + col <= t
        s = jnp.where(mask, _bdot_t(q, k), NEG)
        m_new = jnp.maximum(m, jnp.max(s, axis=-1, keepdims=True))
        e = jnp.where(mask, jnp.exp(s - m_new), 0.0)
        alpha = jnp.exp(m - m_new)
        return m_new, alpha * l + jnp.sum(e, axis=-1, keepdims=True), alpha * acc + _bdot(e, k[:, 0:KV_LORA])

    init = (jnp.full((N_HEADS * tq, 1), NEG, F32), jnp.zeros((N_HEADS * tq, 1), F32),
            jnp.zeros((N_HEADS * tq, KV_LORA), F32))
    m, l, acc = lax.fori_loop(0, i + 1, body, init)
    o_ref[...] = _mla_out(acc / l, wvb_ref, tq)


def _mla_prompt(q, keys, wvb, b, tq):
    m = keys.shape[0]
    l = m // b
    assert l % tq == 0
    nq = l // tq
    return pl.pallas_call(
        functools.partial(_mla_prompt_kernel, tq=tq), grid=(b, nq),
        in_specs=[pl.BlockSpec((N_HEADS, tq, 2 * LANES), lambda bi, i: (0, bi * nq + i, 0)),
                  pl.BlockSpec((l, 2 * LANES), lambda bi, i: (bi, 0)), _full(wvb.shape)],
        out_specs=pl.BlockSpec((tq, GROUP_W), lambda bi, i: (bi * nq + i, 0)),
        out_shape=jax.ShapeDtypeStruct((m, GROUP_W), F32),
        compiler_params=_cparams(("parallel", "parallel")))(q, keys, wvb)


def _page_copies(table_ref, windows, sem_ref, b, slot, n_pages):
    return [pltpu.make_async_copy(src, dst, sem_ref.at[slot])
            for j in range(n_pages) for src, dst in windows(table_ref[b, j], slot, j)]


def _paged_fetch(table_ref, windows, sem_ref, n_pages):
    b = pl.program_id(0)
    slot = b % 2

    @pl.when(b == 0)
    def _():
        for c in _page_copies(table_ref, windows, sem_ref, b, slot, n_pages):
            c.start()

    @pl.when(b + 1 < pl.num_programs(0))
    def _():
        for c in _page_copies(table_ref, windows, sem_ref, b + 1, 1 - slot, n_pages):
            c.start()

    for c in _page_copies(table_ref, windows, sem_ref, b, slot, n_pages):
        c.wait()
    return slot


def _mla_sample_kernel(table_ref, q_ref, new_ref, pool_ref, wvb_ref, o_ref, buf_ref, sem_ref, *, layer, n_pages, lq):
    page = buf_ref.shape[2] // n_pages
    windows = lambda pid, slot, j: [(pool_ref.at[layer, pid], buf_ref.at[slot, :, pl.ds(j * page, page)])]
    slot = _paged_fetch(table_ref, windows, sem_ref, n_pages)
    past_t = buf_ref[slot].astype(BF16)
    new = new_ref[...].astype(BF16)
    rows = N_HEADS * lq
    q = q_ref[...].reshape(rows, 2 * LANES)[:, 0:C_MLA].astype(BF16)
    s_past = jnp.dot(q, past_t, preferred_element_type=F32)
    qi = lax.broadcasted_iota(jnp.int32, (rows, lq), 0) % lq
    mask = lax.broadcasted_iota(jnp.int32, (rows, lq), 1) <= qi
    s_new = jnp.where(mask, _bdot_t(q, new), NEG)
    m = jnp.maximum(jnp.max(s_past, axis=-1, keepdims=True), jnp.max(s_new, axis=-1, keepdims=True))
    e_past = jnp.exp(s_past - m)
    e_new = jnp.where(mask, jnp.exp(s_new - m), 0.0)
    den = jnp.sum(e_past, axis=-1, keepdims=True) + jnp.sum(e_new, axis=-1, keepdims=True)
    o_lat = (_bdot_t(e_past, past_t[0:KV_LORA, :]) + _bdot(e_new, new[:, 0:KV_LORA])) / den
    o_ref[...] = _mla_out(o_lat, wvb_ref, lq)


def _mla_sample(q, new_rows, pool_t, page_table, wvb, layer, row0):
    b, n_pages = page_table.shape
    page = pool_t.shape[3]
    lq = (q.shape[1] - row0) // b
    assert row0 % lq == 0 and page % LANES == 0
    blk0 = row0 // lq
    grid_spec = pltpu.PrefetchScalarGridSpec(
        num_scalar_prefetch=1, grid=(b,),
        in_specs=[pl.BlockSpec((N_HEADS, lq, 2 * LANES), lambda i, tbl: (0, blk0 + i, 0)),
                  pl.BlockSpec((lq, C_MLA), lambda i, tbl: (blk0 + i, 0)),
                  pl.BlockSpec(memory_space=pl.ANY),
                  pl.BlockSpec(wvb.shape, lambda i, tbl: (0, 0, 0))],
        out_specs=pl.BlockSpec((lq, GROUP_W), lambda i, tbl: (i, 0)),
        scratch_shapes=[pltpu.VMEM((2, C_MLA, n_pages * page), F32), pltpu.SemaphoreType.DMA((2,))])
    return pl.pallas_call(
        functools.partial(_mla_sample_kernel, layer=layer, n_pages=n_pages, lq=lq), grid_spec=grid_spec,
        out_shape=jax.ShapeDtypeStruct((b * lq, GROUP_W), F32),
        compiler_params=_cparams(("arbitrary",)))(page_table, q, new_rows, pool_t, wvb)


def _gelu_tanh(x):
    return 0.5 * x * (1.0 + jnp.tanh(0.7978845608028654 * (x + 0.044715 * x * x * x)))


def _compress(a, wexp_ref, pos_ref, b1_ref, w2k_ref, w2v_ref, b2_ref):
    n_ch = a.shape[0]
    c = jnp.dot(pos_ref[...], wexp_ref[...], preferred_element_type=F32)
    up = lambda t: pltpu.roll(t, n_ch - 1, axis=0)
    hk = a[:, 0:LANES] + up(a[:, LANES:2 * LANES]) + c[0:1, 0:LANES] + c[1:2, LANES:2 * LANES] + b1_ref[:, 0:LANES]
    hv = (a[:, 2 * LANES:3 * LANES] + up(a[:, 3 * LANES:4 * LANES]) + c[0:1, 2 * LANES:3 * LANES]
          + c[1:2, 3 * LANES:4 * LANES] + b1_ref[:, LANES:2 * LANES])
    out = _bdot(_gelu_tanh(hk), w2k_ref[...]) + _bdot(_gelu_tanh(hv), w2v_ref[...]) + b2_ref[...]
    valid = lax.broadcasted_iota(jnp.int32, out.shape, 0) < n_ch - 1
    return jnp.where(valid, out, 0.0)


def _compress_kernel(ch_ref, wexp_ref, pos_ref, b1_ref, w2k_ref, w2v_ref, b2_ref, o_ref):
    a = jnp.dot(ch_ref[...].astype(BF16), wexp_ref[...], preferred_element_type=F32)
    o_ref[...] = _compress(a, wexp_ref, pos_ref, b1_ref, w2k_ref, w2v_ref, b2_ref)


def _stack_heads(q, scale):
    low = lax.broadcasted_iota(jnp.int32, (q.shape[0], LANES), 1) < HEAD_DIM
    parts = []
    for h in range(N_HEADS):
        slab = q[:, (h // 2) * LANES:(h // 2 + 1) * LANES]
        if h % 2:
            slab = pltpu.roll(slab, HEAD_DIM, axis=1)
        parts.append(jnp.where(low, slab * scale, 0.0))
    return jnp.concatenate(parts, axis=0).astype(BF16)


def _softmax_parts(parts):
    m = None
    for s, mask in parts:
        mx = jnp.max(jnp.where(mask, s, NEG), axis=-1, keepdims=True)
        m = mx if m is None else jnp.maximum(m, mx)
    es = [jnp.where(mask, jnp.exp(jnp.where(mask, s, NEG) - m), 0.0) for s, mask in parts]
    den = sum(jnp.sum(e, axis=-1, keepdims=True) for e in es)
    return es, jnp.where(den > 0.0, den, 1.0)


def _select_blocks(p_c, t, n_cb, n_sb, nq):
    ncbp = p_c.shape[1]
    nsbp = -(-n_sb // LANES) * LANES
    psum = p_c[0:nq] + p_c[nq:2 * nq] + p_c[2 * nq:3 * nq] + p_c[3 * nq:4 * nq]
    ci = lax.broadcasted_iota(jnp.int32, (ncbp, nsbp), 0)
    sj = lax.broadcasted_iota(jnp.int32, (ncbp, nsbp), 1)
    overlap = ((CMP_STRIDE * ci < SLC_BLOCK * (sj + 1)) & (CMP_STRIDE * ci + CMP_LEN > SLC_BLOCK * sj)
               & (ci < n_cb) & (sj < n_sb)).astype(BF16)
    hi = psum.astype(BF16)
    lo = (psum - hi.astype(F32)).astype(BF16)
    imp = jnp.dot(hi, overlap, preferred_element_type=F32) + jnp.dot(lo, overlap, preferred_element_type=F32)
    j = lax.broadcasted_iota(jnp.int32, (nq, nsbp), 1)
    cur = t // SLC_BLOCK
    forced = (j == 0) | (j == cur) | (j == cur - 1)
    score = jnp.where(forced, FORCE_SCORE, jnp.where(j <= cur, imp, -1.0))
    score = jnp.where(j < n_sb, score, -3e38)
    sel = jnp.zeros((nq, nsbp), F32)
    for _ in range(min(N_SELECT, n_sb)):
        mx = jnp.max(score, axis=-1, keepdims=True)
        first = jnp.min(jnp.where(score == mx, j, nsbp), axis=-1, keepdims=True)
        hit = j == first
        sel = jnp.where(hit, 1.0, sel)
        score = jnp.where(hit, -3e38, score)
    return sel.astype(BF16)


def _tile4(x):
    return jnp.concatenate([x] * N_HEADS, axis=0)


def _gate_combine(gates, o_c, o_s, o_w, o_ref, nq):
    for h in range(N_HEADS):
        rows = slice(h * nq, (h + 1) * nq)
        g = lambda c: gates[:, 3 * h + c:3 * h + c + 1]
        o = g(0) * o_c[rows] + g(1) * o_s[rows] + g(2) * o_w[rows]
        o_ref[:, h * HEAD_DIM:(h + 1) * HEAD_DIM] = o[:, HEAD_DIM:LANES]


def _nsa_prompt_kernel(q_ref, g_ref, cmp_ref, slc_ref, win_ref, o_ref, slc_s, win_s, *, tq, l, tks):
    i = pl.program_id(1)

    @pl.when(i == 0)
    def _():
        slc_s[...] = slc_ref[...].astype(BF16)
        win_s[...] = win_ref[...].astype(BF16)

    rows = N_HEADS * tq
    n_cb, n_sb = l // CMP_STRIDE - 1, l // SLC_BLOCK
    q = _stack_heads(q_ref[...], HEAD_DIM ** -0.5)
    tq1 = i * tq + lax.broadcasted_iota(jnp.int32, (tq, 1), 0)
    t = _tile4(tq1)
    span = WINDOW + tq
    w0 = pl.multiple_of(jnp.maximum(i * tq - WINDOW, 0), tq)
    kvw = win_s[pl.ds(w0, span), :]
    wpos = w0 + lax.broadcasted_iota(jnp.int32, (rows, span), 1)
    (e_w,), den = _softmax_parts([(_bdot_t(q, kvw), (wpos <= t) & (wpos > t - WINDOW))])
    o_w = _bdot(e_w, kvw) / den
    kcvc = cmp_ref[...].astype(BF16)
    n = lax.broadcasted_iota(jnp.int32, (rows, kcvc.shape[0]), 1)
    (e_c,), den = _softmax_parts([(_bdot_t(q, kcvc), (CMP_STRIDE * n + CMP_LEN <= t + 1) & (n < n_cb))])
    p_c = e_c / den
    o_c = _bdot(p_c, kcvc)
    sel = _select_blocks(p_c, tq1, n_cb, n_sb, tq)
    blk = lax.broadcasted_iota(jnp.int32, (sel.shape[1], tks), 0)
    kcol = lax.broadcasted_iota(jnp.int32, (sel.shape[1], tks), 1)
    col = lax.broadcasted_iota(jnp.int32, (rows, tks), 1)

    def body(kt, carry):
        m, lsum, acc = carry
        k0 = pl.multiple_of(kt * tks, tks)
        kv = slc_s[pl.ds(k0, tks), :]
        expand = (blk == (k0 + kcol) // SLC_BLOCK).astype(BF16)
        chosen = _tile4(jnp.dot(sel, expand, preferred_element_type=F32)) > 0.5
        mask = chosen & (k0 + col <= t)
        s = jnp.where(mask, _bdot_t(q, kv), NEG)
        m_new = jnp.maximum(m, jnp.max(s, axis=-1, keepdims=True))
        e = jnp.where(mask, jnp.exp(s - m_new), 0.0)
        alpha = jnp.exp(m - m_new)
        return m_new, alpha * lsum + jnp.sum(e, axis=-1, keepdims=True), alpha * acc + _bdot(e, kv)

    init = (jnp.full((rows, 1), NEG, F32), jnp.zeros((rows, 1), F32), jnp.zeros((rows, LANES), F32))
    _, lsum, acc = lax.fori_loop(0, ((i + 1) * tq + tks - 1) // tks, body, init)
    o_s = acc / jnp.where(lsum > 0.0, lsum, 1.0)
    _gate_combine(jax.nn.sigmoid(g_ref[...]), o_c, o_s, o_w, o_ref, tq)


def _nsa_prompt(p_nsa, lw, b, l, tq):
    m = b * l
    n_ch = l // CMP_STRIDE
    tks = 4 * SLC_BLOCK
    assert l % tq == 0 and l >= WINDOW + tq and l % tks == 0 and tq % SLC_BLOCK == 0 and WINDOW % tq == 0
    ch = p_nsa[:m, GROUP_W:GROUP_W + LANES].reshape(m // CMP_STRIDE, CMP_STRIDE * LANES)
    kcvc = pl.pallas_call(
        _compress_kernel, grid=(b,),
        in_specs=[pl.BlockSpec((n_ch, CMP_STRIDE * LANES), lambda bi: (bi, 0)), _full(lw['cmp_wexp_p'].shape),
                  _full(lw['cmp_pos_p'].shape), _full((1, 2 * LANES)), _full((LANES, LANES)), _full((LANES, LANES)),
                  _full((1, LANES))],
        out_specs=pl.BlockSpec((n_ch, LANES), lambda bi: (bi, 0)),
        out_shape=jax.ShapeDtypeStruct((b * n_ch, LANES), F32),
        compiler_params=_cparams(("parallel",)))(
            ch, lw['cmp_wexp_p'], lw['cmp_pos_p'], lw['cmp_b1'], lw['cmp_w2k'], lw['cmp_w2v'], lw['cmp_b2'])
    nq = l // tq
    slab = lambda c: pl.BlockSpec((l, LANES), lambda bi, i: (bi, c))
    return pl.pallas_call(
        functools.partial(_nsa_prompt_kernel, tq=tq, l=l, tks=tks), grid=(b, nq),
        in_specs=[pl.BlockSpec((tq, GROUP_W), lambda bi, i: (bi * nq + i, 0)),
                  pl.BlockSpec((tq, LANES), lambda bi, i: (bi * nq + i, NSA_W // LANES - 1)),
                  pl.BlockSpec((n_ch, LANES), lambda bi, i: (bi, 0)), slab(3), slab(4)],
        out_specs=pl.BlockSpec((tq, GROUP_W), lambda bi, i: (bi * nq + i, 0)),
        out_shape=jax.ShapeDtypeStruct((m, GROUP_W), F32),
        scratch_shapes=[pltpu.VMEM((l, LANES), BF16), pltpu.VMEM((l, LANES), BF16)],
        compiler_params=_cparams(("parallel", "arbitrary")))(p_nsa, p_nsa, kcvc, p_nsa, p_nsa)


def _nsa_sample_kernel(table_ref, q_ref, g_ref, nslc_ref, nwin_ref, win_ref, pool_ref, wexp_ref, pos_ref, b1_ref,
                       w2k_ref, w2v_ref, b2_ref, o_ref, wout_ref, cmp_buf, slc_buf, sem_ref, *, layer, n_pages, lq, past,
                       tks):
    page = past // n_pages

    def windows(pid, slot, j):
        return [(pool_ref.at[layer, pid, :, pl.ds(c * LANES, LANES)], buf.at[slot, pl.ds(j * page, page), :])
                for c, buf in enumerate((cmp_buf, slc_buf))]

    slot = _paged_fetch(table_ref, windows, sem_ref, n_pages)
    n_ch = past // CMP_STRIDE
    rows = N_HEADS * lq
    n_cb = (past + lq) // CMP_STRIDE - 1
    n_sb = -(-(past + lq) // SLC_BLOCK)
    q = _stack_heads(q_ref[...], HEAD_DIM ** -0.5)
    tq1 = past + lax.broadcasted_iota(jnp.int32, (lq, 1), 0)
    t = _tile4(tq1)
    win = win_ref[...]
    nwin = nwin_ref[...]
    wb = win.shape[0]
    qi = lax.broadcasted_iota(jnp.int32, (rows, lq), 0) % lq
    causal_new = lax.broadcasted_iota(jnp.int32, (rows, lq), 1) <= qi
    wpos = past - wb + lax.broadcasted_iota(jnp.int32, (rows, wb), 1)
    es, den = _softmax_parts([(_bdot_t(q, win), (wpos <= t) & (wpos > t - WINDOW) & (wpos >= 0)),
                              (_bdot_t(q, nwin), causal_new)])
    o_w = (_bdot(es[0], win) + _bdot(es[1], nwin)) / den
    wout_ref[0:wb - lq, :] = win[lq:wb, :]
    wout_ref[wb - lq:wb, :] = nwin
    a = jnp.zeros((n_ch, 4 * CMP_HID), F32)
    for r in range(CMP_STRIDE):
        x = cmp_buf[slot, pl.ds(r, n_ch, stride=CMP_STRIDE), :]
        a = a + jnp.dot(x.astype(BF16), wexp_ref[r * LANES:(r + 1) * LANES, :], preferred_element_type=F32)
    kcvc = _compress(a, wexp_ref, pos_ref, b1_ref, w2k_ref, w2v_ref, b2_ref).astype(BF16)
    n = lax.broadcasted_iota(jnp.int32, (rows, n_ch), 1)
    (e_c,), den = _softmax_parts([(_bdot_t(q, kcvc), (CMP_STRIDE * n + CMP_LEN <= t + 1) & (n < n_cb))])
    p_c = e_c / den
    o_c = _bdot(p_c, kcvc)
    sel = _select_blocks(p_c, tq1, n_cb, n_sb, lq)
    nsbp = sel.shape[1]
    blk = lax.broadcasted_iota(jnp.int32, (nsbp, tks), 0)
    kcol = lax.broadcasted_iota(jnp.int32, (nsbp, tks), 1)
    slabs, parts = [], []
    for kt in range(past // tks):
        kv = slc_buf[slot, pl.ds(kt * tks, tks), :].astype(BF16)
        expand = (blk == (kt * tks + kcol) // SLC_BLOCK).astype(BF16)
        chosen = _tile4(jnp.dot(sel, expand, preferred_element_type=F32)) > 0.5
        slabs.append(kv)
        parts.append((_bdot_t(q, kv), chosen))
    nslc = nslc_ref[...].astype(BF16)
    new_blk = past // SLC_BLOCK
    sel_new = _tile4(sel[:, new_blk:new_blk + 1].astype(F32)) > 0.5
    parts.append((_bdot_t(q, nslc), causal_new & sel_new))
    es, den = _softmax_parts(parts)
    acc = _bdot(es[-1], nslc)
    for e, kv in zip(es[:-1], slabs):
        acc = acc + _bdot(e, kv)
    o_s = acc / den
    _gate_combine(jax.nn.sigmoid(g_ref[...]), o_c, o_s, o_w, o_ref, lq)


def _nsa_sample(p_nsa, row0, pool, page_table, win_cache, lw, layer):
    b, n_pages = page_table.shape
    page, c_in = pool.shape[2:]
    past = n_pages * page
    lq = (p_nsa.shape[0] - row0) // b
    wb = win_cache.shape[1]
    tks = min(past, 1024)
    assert row0 % lq == 0 and page % CMP_STRIDE == 0 and lq < CMP_STRIDE and lq % 8 == 0 and past % tks == 0
    blk0 = row0 // lq
    row = lambda w, c: pl.BlockSpec((lq, w), lambda i, tbl: (blk0 + i, c))
    const = lambda a: pl.BlockSpec(a.shape, lambda i, tbl: (0,) * a.ndim)
    weights = [lw['cmp_wexp_p'], lw['cmp_pos_p'], lw['cmp_b1'], lw['cmp_w2k'], lw['cmp_w2v'], lw['cmp_b2']]
    grid_spec = pltpu.PrefetchScalarGridSpec(
        num_scalar_prefetch=1, grid=(b,),
        in_specs=[row(GROUP_W, 0), row(LANES, NSA_W // LANES - 1), row(LANES, 3), row(LANES, 4),
                  pl.BlockSpec((None, wb, LANES), lambda i, tbl: (i, 0, 0)),
                  pl.BlockSpec(memory_space=pl.ANY)] + [const(w) for w in weights],
        out_specs=[pl.BlockSpec((lq, GROUP_W), lambda i, tbl: (i, 0)),
                   pl.BlockSpec((None, wb, LANES), lambda i, tbl: (i, 0, 0))],
        scratch_shapes=[pltpu.VMEM((2, past, LANES), F32), pltpu.VMEM((2, past, LANES), F32),
                        pltpu.SemaphoreType.DMA((2,))])
    return pl.pallas_call(
        functools.partial(_nsa_sample_kernel, layer=layer, n_pages=n_pages, lq=lq, past=past, tks=tks),
        grid_spec=grid_spec,
        out_shape=[jax.ShapeDtypeStruct((b * lq, GROUP_W), F32), jax.ShapeDtypeStruct(win_cache.shape, F32)],
        compiler_params=_cparams(("arbitrary",)))(page_table, p_nsa, p_nsa, p_nsa, p_nsa, win_cache, pool, *weights)


def _pad_cols(w, width):
    return jnp.pad(w, ((0, 0), (0, width - w.shape[1])))


def _rot_cols(w):
    half = QK_ROPE // 2
    return jnp.concatenate([-w[:, half:], w[:, :half]], axis=1)


def _expand_cmp_w1(w1, channels):
    half = CMP_STRIDE * HEAD_DIM
    out = jnp.zeros((CMP_STRIDE, channels, 4 * CMP_HID), F32)
    for br in range(2):
        for part in range(2):
            blk = w1[br, part * half:(part + 1) * half].reshape(CMP_STRIDE, HEAD_DIM, CMP_HID)
            c0 = (2 * br + part) * CMP_HID
            out = out.at[:, br * HEAD_DIM:(br + 1) * HEAD_DIM, c0:c0 + CMP_HID].set(blk)
    return out.reshape(CMP_STRIDE * channels, 4 * CMP_HID).astype(BF16)


def _expand_cmp_pos(pos, channels):
    out = jnp.zeros((8, CMP_STRIDE, channels), F32)
    for br in range(2):
        for part in range(2):
            out = out.at[part, :, br * HEAD_DIM:(br + 1) * HEAD_DIM].set(
                pos[br, part * CMP_STRIDE:(part + 1) * CMP_STRIDE])
    return out.reshape(8, CMP_STRIDE * channels).astype(BF16)


def _layer_weights(w, l):
    nsa0 = RW_COLS
    mla0 = nsa0 + GROUP_W + 6 * HEAD_DIM + 3 * N_HEADS
    conv0 = mla0 + 2 * LANES + QK_ROPE
    w_in = w['w_in'][l]
    k_rope = w_in[:, mla0 + 2 * LANES:conv0]
    w_in_ext = jnp.concatenate([
        w_in[:, :nsa0], _pad_cols(w_in[:, nsa0:mla0], NSA_W), w_in[:, mla0:mla0 + 2 * LANES],
        _pad_cols(k_rope, LANES), _pad_cols(_rot_cols(k_rope), LANES), w_in[:, conv0:]], axis=1).astype(BF16)
    d_hd = QK_NOPE + QK_ROPE
    wqb = w['mla_w_qb'][l].reshape(-1, N_HEADS, d_hd).transpose(1, 0, 2)
    wq_rope = wqb[:, :, QK_NOPE:]
    pad3 = lambda a: jnp.pad(a, ((0, 0), (0, 0), (0, LANES - a.shape[2]))).astype(BF16)
    wvb = jnp.zeros((N_HEADS, KV_LORA, GROUP_W), F32)
    for h in range(N_HEADS):
        wvb = wvb.at[h, :, h * HEAD_DIM:(h + 1) * HEAD_DIM].set(w['mla_w_vb'][l][:, h, :])
    w2 = w['cmp_w2'][l]
    return dict(
        norms=w['norms'][l][:, None, :],
        ffn_wg=w['ffn_w_gate'][l].astype(BF16), ffn_wu=w['ffn_w_up'][l].astype(BF16),
        ffn_wd=w['ffn_w_down'][l].astype(BF16),
        w_in_ext=w_in_ext, w_out=w['w_out'][l].astype(BF16),
        rw_mu=w['rw_mu'][l][None, :], rw_vec=w['rw_vec'][l], rw_w2=w['rw_w2'][l].astype(BF16),
        rw_a2=w['rw_a2'][l].astype(BF16), rw_g2=w['rw_g2'][l].astype(BF16),
        cmp_wexp_p=_expand_cmp_w1(w['cmp_w1'][l], LANES), cmp_pos_p=_expand_cmp_pos(w['cmp_pos'][l], LANES),
        cmp_b1=w['cmp_b1'][l].reshape(1, 2 * CMP_HID),
        cmp_w2k=_pad_cols(w2[0], LANES).astype(BF16),
        cmp_w2v=jnp.pad(w2[1], ((0, 0), (HEAD_DIM, 0))).astype(BF16),
        cmp_b2=w['cmp_b2'][l].reshape(1, 2 * HEAD_DIM),
        mla_q_norm=w['mla_q_norm'][l][None, :], mla_kv_norm=w['mla_kv_norm'][l][None, :],
        mla_wq_nope=wqb[:, :, :QK_NOPE].astype(BF16), mla_wq_rope=pad3(wq_rope),
        mla_wq_rope_rot=pad3(jnp.concatenate([-wq_rope[:, :, QK_ROPE // 2:], wq_rope[:, :, :QK_ROPE // 2]], axis=2)),
        mla_wkb=w['mla_w_kb'][l].transpose(1, 2, 0).astype(BF16), mla_wvb=wvb.astype(BF16),
        conv_dw=w['conv_dw'][l], conv_vec=w['conv_vec'][l], conv_pw=w['conv_pw'][l].astype(BF16),
        x_wq=w['x_wq'][l].astype(BF16), x_wkv=w['x_wkv'][l].astype(BF16), x_wo=w['x_wo'][l].astype(BF16))


def _rope_tables(pos):
    half = QK_ROPE // 2
    inv = ROPE_THETA ** (-jnp.arange(half, dtype=F32) / half)
    ang = pos.astype(F32)[:, None] * inv
    tile = lambda a: jnp.tile(a, (1, LANES // half))
    return tile(jnp.cos(ang)), tile(jnp.sin(ang))


def _pick(n, *cands):
    for c in cands:
        if n % c == 0:
            return c
    return n


def kernel(x_prompt, x_sample, cache_mla, cache_nsa, cache_nsa_win, cache_mem, state_rwkv, state_rwkv_shift,
           state_conv, page_table, mem_prompt, norms, ffn_w_gate, ffn_w_up, ffn_w_down, w_in, w_out, rw_mu, rw_vec,
           rw_w2, rw_a2, rw_g2, cmp_pos, cmp_w1, cmp_b1, cmp_w2, cmp_b2, mla_q_norm, mla_kv_norm, mla_w_qb, mla_w_kb,
           mla_w_vb, conv_dw, conv_vec, conv_pw, x_wq, x_wkv, x_wo, final_norm):
    w = dict(norms=norms, ffn_w_gate=ffn_w_gate, ffn_w_up=ffn_w_up, ffn_w_down=ffn_w_down, w_in=w_in, w_out=w_out,
             rw_mu=rw_mu, rw_vec=rw_vec, rw_w2=rw_w2, rw_a2=rw_a2, rw_g2=rw_g2, cmp_pos=cmp_pos, cmp_w1=cmp_w1,
             cmp_b1=cmp_b1, cmp_w2=cmp_w2, cmp_b2=cmp_b2, mla_q_norm=mla_q_norm, mla_kv_norm=mla_kv_norm,
             mla_w_qb=mla_w_qb, mla_w_kb=mla_w_kb, mla_w_vb=mla_w_vb, conv_dw=conv_dw, conv_vec=conv_vec,
             conv_pw=conv_pw, x_wq=x_wq, x_wkv=x_wkv, x_wo=x_wo)
    depth = norms.shape[0]
    bp, lp, d = x_prompt.shape
    bs, ls, _ = x_sample.shape
    mp, ms = bp * lp, bs * ls
    m = mp + ms
    n_pages, page = page_table.shape[1], cache_mla.shape[2]
    past = n_pages * page
    d_ff = ffn_w_gate.shape[-1]
    chunk = _ff_chunk(d_ff)
    tm = _pick(m, 512, 256, 128, 64, 8)
    n_mem = mem_prompt.shape[1]

    x = jnp.concatenate([x_prompt.reshape(mp, d), x_sample.reshape(ms, d)], axis=0)
    cos_p, sin_p = _rope_tables(jnp.arange(lp, dtype=jnp.int32))
    cos_s, sin_s = _rope_tables(past + jnp.arange(ls, dtype=jnp.int32))
    cos = jnp.concatenate([jnp.tile(cos_p, (bp, 1)), jnp.tile(cos_s, (bs, 1))], axis=0)
    sin = jnp.concatenate([jnp.tile(sin_p, (bp, 1)), jnp.tile(sin_s, (bs, 1))], axis=0)
    mem_rows = mem_prompt.reshape(bp * n_mem, d)
    cache_mla_t = jnp.swapaxes(cache_mla, 2, 3)
    zeros = lambda *s: jnp.zeros(s, F32)
    sp, ss = [], []
    both = lambda a, c: jnp.concatenate([a, c], axis=0)

    for l in range(depth):
        lw = _layer_weights(w, l)
        n = lw['norms']
        ffn_w = lambda i: (lw['ffn_wg'][i], lw['ffn_wu'][i])
        (h1,) = _token_call(functools.partial(_ffn_a_kernel, chunk=chunk), [x], [n[0], *ffn_w(0)], [d_ff], [BF16], tm)
        x1, p_rw, p_nsa, p_mla, p_conv = _token_call(
            _ffn_b_proj_kernel, [x, h1], [lw['ffn_wd'][0], n[1], lw['w_in_ext']],
            [d, RW_COLS, NSA_W, MLA_W, 2 * GROUP_W], [F32] * 5, tm)
        (mem_kv,) = _token_call(_norm_proj_kernel, [mem_rows], [n[3], lw['x_wkv']], [2 * GROUP_W], [F32],
                                _pick(bp * n_mem, 256, 8))
        mem_kv = mem_kv.reshape(bp, n_mem, 2 * GROUP_W)
        o_rw_p, rw_state_p = _rwkv_mix(p_rw[:mp], zeros(bp, RW_COLS), zeros(bp, N_HEADS, HEAD_DIM, HEAD_DIM), lw,
                                       nb=bp, tc=_pick(lp, 256, 8), tl=_pick(lp, 512, 8))
        o_rw_s, rw_state_s = _rwkv_mix(p_rw[mp:], state_rwkv_shift[l], state_rwkv[l], lw,
                                       nb=_pick(bs, 4, 1), tc=ls, tl=ls)
        o_nsa_p = _nsa_prompt(p_nsa, lw, bp, lp, tq=_pick(lp, 256, 128))
        o_nsa_s, win_s = _nsa_sample(p_nsa, mp, cache_nsa, page_table, cache_nsa_win[l], lw, l)
        mla_rows, mla_keys, mla_q = _mla_prep(p_mla, cos, sin, lw, tm)
        o_mla_p = _mla_prompt(mla_q[:, :mp], mla_keys[:mp], lw['mla_wvb'], bp, tq=_pick(lp, 256, 8))
        o_mla_s = _mla_sample(mla_q, mla_rows, cache_mla_t, page_table, lw['mla_wvb'], l, mp)
        o_conv_p, conv_p = _conv_mix(p_conv[:mp], zeros(bp, CONV_W - 1, GROUP_W), lw, tl=_pick(lp, 512, 8))
        o_conv_s, conv_s = _conv_mix(p_conv[mp:], state_conv[l], lw, tl=ls)
        x2, qx = _token_call(
            _mix_out_kernel,
            [x1, both(o_rw_p, o_rw_s), both(o_nsa_p, o_nsa_s), both(o_mla_p, o_mla_s), both(o_conv_p, o_conv_s)],
            [lw['w_out'], n[2], lw['x_wq']], [d, GROUP_W], [F32] * 2, tm)
        ox = both(_cross_attend(qx[:mp], mem_kv, 1, _pick(lp, 512, 8)),
                  _cross_attend(qx[mp:], cache_mem[l], _pick(bs, 8, 1), ls))
        x3, h2 = _token_call(functools.partial(_xo_ffn_a_kernel, chunk=chunk), [x2, ox],
                             [lw['x_wo'], n[4], *ffn_w(1)], [d, d_ff], [F32, BF16], tm)
        (x,) = _token_call(_ffn_b_kernel, [x3, h2], [lw['ffn_wd'][1]], [d], [F32], tm)

        nsa_rows = p_nsa[:, GROUP_W:GROUP_W + 4 * HEAD_DIM]
        keep = min(WINDOW, lp)
        nsa_win_p = p_nsa[:mp, GROUP_W + 4 * HEAD_DIM:GROUP_W + 6 * HEAD_DIM].reshape(bp, lp, 2 * HEAD_DIM)[:, lp - keep:]
        sp.append((mla_rows[:mp].reshape(bp, lp, C_MLA), nsa_rows[:mp].reshape(bp, lp, 4 * HEAD_DIM), nsa_win_p,
                   mem_kv, rw_state_p, p_rw[:mp].reshape(bp, lp, RW_COLS)[:, -1], conv_p))
        ss.append((mla_rows[mp:].reshape(bs, ls, C_MLA), nsa_rows[mp:].reshape(bs, ls, 4 * HEAD_DIM), win_s,
                   rw_state_s, p_rw[mp:].reshape(bs, ls, RW_COLS)[:, -1], conv_s))

    (y,) = _token_call(_norm_kernel, [x], [final_norm[None, :]], [d], [F32], tm)
    stack = lambda states, i: jnp.stack([s[i] for s in states])
    return (y[:mp].reshape(bp, lp, d), y[mp:].reshape(bs, ls, d),
            *(stack(sp, i) for i in range(7)), *(stack(ss, i) for i in range(6)))
```

```python
import functools

import jax
import jax.numpy as jnp
from jax import lax
from jax.experimental import pallas as pl
from jax.experimental.pallas import tpu as pltpu

F32 = jnp.float32
BF16 = jnp.bfloat16
HIGHEST = lax.Precision.HIGHEST

GROUP_W = 256
HEAD_DIM = 64
N_HEADS = 4
LANES = 128
SUBLANES = 8
W_LORA, A_LORA, G_LORA = 32, 32, 64
RW_COLS = 3 * GROUP_W + W_LORA + A_LORA + G_LORA
RW_GN_EPS = 64e-5
CMP_LEN, CMP_STRIDE, CMP_HID = 32, 16, 128
SLC_BLOCK, N_SELECT, WINDOW = 64, 16, 512
KV_LORA, QK_NOPE, QK_ROPE = 128, 64, 32
C_MLA = KV_LORA + QK_ROPE
MLA_SCALE = (QK_NOPE + QK_ROPE) ** -0.5
ROPE_THETA = 10000.0
CONV_W = 31
CONV_LN_EPS = 1e-5
NORM_EPS = 1e-6
NEG = -1e30
FORCE_SCORE = 1e4
NSA_W = 768
MLA_W = 512
VMEM_LIMIT = 56 * 1024 * 1024


def _cparams(sem, vmem=VMEM_LIMIT):
    return pltpu.CompilerParams(dimension_semantics=sem, vmem_limit_bytes=vmem)


def _rms(x, g):
    return x * lax.rsqrt(jnp.mean(x * x, axis=-1, keepdims=True) + NORM_EPS) * g


def _bdot(a, b):
    return jnp.dot(a.astype(BF16), b.astype(BF16), preferred_element_type=F32)


def _bdot_t(a, b):
    return lax.dot_general(a.astype(BF16), b.astype(BF16), (((1,), (1,)), ((), ())), preferred_element_type=F32)


def _seg_ones(n, seg):
    r = lax.broadcasted_iota(jnp.int32, (n, n), 0) // seg
    c = lax.broadcasted_iota(jnp.int32, (n, n), 1) // seg
    return (r == c).astype(F32)


def _full(shape):
    nd = len(shape)
    return pl.BlockSpec(shape, lambda *_: (0,) * nd)


def _swiglu_to(xn, wg_ref, wu_ref, h_ref, chunk):
    for c in range(h_ref.shape[1] // chunk):
        sl = slice(c * chunk, (c + 1) * chunk)
        gate = jnp.dot(xn, wg_ref[:, sl], preferred_element_type=F32)
        up = jnp.dot(xn, wu_ref[:, sl], preferred_element_type=F32)
        h_ref[:, sl] = (gate * jax.nn.sigmoid(gate) * up).astype(BF16)


def _ffn_a_kernel(x_ref, g_ref, wg_ref, wu_ref, h_ref, *, chunk):
    xn = _rms(x_ref[...], g_ref[...]).astype(BF16)
    _swiglu_to(xn, wg_ref, wu_ref, h_ref, chunk)


def _ffn_b_proj_kernel(x_ref, h_ref, wd_ref, g_ref, win_ref, x1_ref, prw_ref, pnsa_ref, pmla_ref, pconv_ref):
    x1 = x_ref[...] + 0.5 * jnp.dot(h_ref[...], wd_ref[...], preferred_element_type=F32)
    x1_ref[...] = x1
    xn = _rms(x1, g_ref[...]).astype(BF16)
    off = 0
    for ref in (prw_ref, pnsa_ref, pmla_ref, pconv_ref):
        w = ref.shape[1]
        ref[...] = jnp.dot(xn, win_ref[:, off:off + w], preferred_element_type=F32)
        off += w


def _mix_out_kernel(x_ref, orw_ref, onsa_ref, omla_ref, oconv_ref, wo_ref, g_ref, wq_ref, x2_ref, q_ref):
    acc = x_ref[...]
    for i, ref in enumerate((orw_ref, onsa_ref, omla_ref, oconv_ref)):
        acc = acc + _bdot(ref[...], wo_ref[i * GROUP_W:(i + 1) * GROUP_W, :])
    x2_ref[...] = acc
    q_ref[...] = _bdot(_rms(acc, g_ref[...]), wq_ref[...]) * (HEAD_DIM ** -0.5)


def _xo_ffn_a_kernel(x_ref, ox_ref, wo_ref, g_ref, wg_ref, wu_ref, x3_ref, h_ref, *, chunk):
    x3 = x_ref[...] + _bdot(ox_ref[...], wo_ref[...])
    x3_ref[...] = x3
    _swiglu_to(_rms(x3, g_ref[...]).astype(BF16), wg_ref, wu_ref, h_ref, chunk)


def _ffn_b_kernel(x_ref, h_ref, wd_ref, o_ref):
    o_ref[...] = x_ref[...] + 0.5 * jnp.dot(h_ref[...], wd_ref[...], preferred_element_type=F32)


def _norm_kernel(x_ref, g_ref, o_ref):
    o_ref[...] = _rms(x_ref[...], g_ref[...])


def _norm_proj_kernel(x_ref, g_ref, w_ref, o_ref):
    o_ref[...] = _bdot(_rms(x_ref[...], g_ref[...]), w_ref[...])


def _token_call(body, tiled_in, full_in, out_widths, out_dtypes, tm):
    m = tiled_in[0].shape[0]
    assert m % tm == 0
    in_specs = [pl.BlockSpec((tm, a.shape[1]), lambda i: (i, 0)) for a in tiled_in]
    in_specs += [_full(a.shape) for a in full_in]
    out_specs = [pl.BlockSpec((tm, w), lambda i: (i, 0)) for w in out_widths]
    out_shape = [jax.ShapeDtypeStruct((m, w), dt) for w, dt in zip(out_widths, out_dtypes)]
    return pl.pallas_call(body, grid=(m // tm,), in_specs=in_specs, out_specs=out_specs, out_shape=out_shape,
                          compiler_params=_cparams(("parallel",)))(*tiled_in, *full_in)


def _ff_chunk(d_ff):
    return 256 if d_ff % 256 == 0 else LANES


def _xattn_kernel(q_ref, kv_ref, o_ref, *, groups, lq):
    for g in range(groups):
        rows = slice(g * lq, (g + 1) * lq)
        for h in range(N_HEADS):
            cols = slice(h * HEAD_DIM, (h + 1) * HEAD_DIM)
            k = kv_ref[g, :, cols]
            v = kv_ref[g, :, GROUP_W + h * HEAD_DIM:GROUP_W + (h + 1) * HEAD_DIM]
            s = _bdot_t(q_ref[rows, cols], k)
            e = jnp.exp(s - jnp.max(s, axis=-1, keepdims=True))
            p = e / jnp.sum(e, axis=-1, keepdims=True)
            o_ref[rows, cols] = _bdot(p, v)


def _xattn_stacked_kernel(q_ref, kv_ref, o_ref, *, groups, lq):
    head = lax.broadcasted_iota(jnp.int32, (lq, GROUP_W), 1) // HEAD_DIM
    for g in range(groups):
        rows = slice(g * lq, (g + 1) * lq)
        q = q_ref[rows, :]
        qs = jnp.concatenate([jnp.where(head == h, q, 0.0) for h in range(N_HEADS)], axis=0)
        s = _bdot_t(qs, kv_ref[g, :, 0:GROUP_W])
        e = jnp.exp(s - jnp.max(s, axis=-1, keepdims=True))
        p = e / jnp.sum(e, axis=-1, keepdims=True)
        o = _bdot(p, kv_ref[g, :, GROUP_W:2 * GROUP_W])
        out = jnp.zeros((lq, GROUP_W), F32)
        for h in range(N_HEADS):
            out = jnp.where(head == h, o[h * lq:(h + 1) * lq], out)
        o_ref[rows, :] = out


def _cross_attend(q, row0, l, mem_kv, groups, lq_tile):
    b, n_mem, _ = mem_kv.shape
    if groups > 1:
        assert lq_tile == l and b % groups == 0 and row0 % (groups * l) == 0
        blk0 = row0 // (groups * l)
        grid = (b // groups,)
        q_spec = pl.BlockSpec((groups * l, GROUP_W), lambda i: (blk0 + i, 0))
        o_spec = pl.BlockSpec((groups * l, GROUP_W), lambda i: (i, 0))
        kv_spec = pl.BlockSpec((groups, n_mem, 2 * GROUP_W), lambda i: (i, 0, 0))
        body = _xattn_stacked_kernel
    else:
        assert l % lq_tile == 0 and row0 % lq_tile == 0
        nq = l // lq_tile
        blk0 = row0 // lq_tile
        grid = (b, nq)
        q_spec = pl.BlockSpec((lq_tile, GROUP_W), lambda bi, i: (blk0 + bi * nq + i, 0))
        o_spec = pl.BlockSpec((lq_tile, GROUP_W), lambda bi, i: (bi * nq + i, 0))
        kv_spec = pl.BlockSpec((1, n_mem, 2 * GROUP_W), lambda bi, i: (bi, 0, 0))
        body = _xattn_kernel
    return pl.pallas_call(
        functools.partial(body, groups=groups, lq=lq_tile), grid=grid, in_specs=[q_spec, kv_spec],
        out_specs=o_spec, out_shape=jax.ShapeDtypeStruct((b * l, GROUP_W), F32),
        compiler_params=_cparams(("parallel",) * len(grid)))(q, mem_kv)


def _rw_prep_kernel(p_ref, prev_ref, mu_ref, vec_ref, w2_ref, a2_ref, g2_ref,
                    rp_ref, nkk_ref, w_ref, kka_ref, k_ref, v_ref, ovk_ref, g_ref, bonus_ref, carry_ref, *, seqs):
    p = p_ref[...]
    tl = p.shape[0]
    row = lax.broadcasted_iota(jnp.int32, p.shape, 0)
    if seqs == 1:
        @pl.when(pl.program_id(1) == 0)
        def _():
            carry_ref[...] = prev_ref[0]

        prev = jnp.where(row == 0, carry_ref[...], pltpu.roll(p, 1, axis=0))
        carry_ref[...] = p[tl - 1:tl, :]
    else:
        seq = tl // seqs
        first = jnp.broadcast_to(prev_ref[...], (seqs, seq, p.shape[1])).reshape(tl, p.shape[1])
        prev = jnp.where(row % seq == 0, first, pltpu.roll(p, 1, axis=0))
    xs = p + (prev - p) * mu_ref[...]
    r, k, v = xs[:, 0:GROUP_W], xs[:, GROUP_W:2 * GROUP_W], xs[:, 2 * GROUP_W:3 * GROUP_W]
    o = 3 * GROUP_W
    xw, xa, xg = xs[:, o:o + W_LORA], xs[:, o + W_LORA:o + W_LORA + A_LORA], xs[:, o + W_LORA + A_LORA:]
    w0, a0, k_k, k_a, r_k = (vec_ref[i:i + 1, :] for i in range(5))
    z = -(w0 + _bdot(jnp.tanh(xw), w2_ref[...]))
    softplus = jnp.maximum(z, 0.0) + jnp.log1p(jnp.exp(-jnp.abs(z)))
    decay = jnp.exp(-jnp.exp(-softplus - 0.5))
    a = jax.nn.sigmoid(a0 + _bdot(xa, a2_ref[...]))
    g_ref[...] = _bdot(jax.nn.sigmoid(xg), g2_ref[...])
    ones = _seg_ones(GROUP_W, HEAD_DIM)
    seg = lambda t: jnp.dot(t, ones, precision=HIGHEST, preferred_element_type=F32)
    kk = k * k_k
    kk = kk / jnp.maximum(jnp.sqrt(seg(kk * kk)), 1e-12)
    k_eff = k * (1.0 + (a - 1.0) * k_a)
    nkk = -kk
    kka = kk * a
    rp_ref[...] = decay * r + nkk * seg(kka * r)
    ovk_ref[...] = v * seg(k_eff * r)
    bonus_ref[...] = seg(r * k_eff * r_k) * v
    nkk_ref[...] = nkk
    w_ref[...] = decay
    kka_ref[...] = kka
    k_ref[...] = k_eff
    v_ref[...] = v


def _rw_scan_kernel(rp_ref, nkk_ref, w_ref, kka_ref, k_ref, v_ref, ovk_ref, g_ref, bonus_ref, s0_ref, vec_ref,
                    out_ref, sfin_ref, s_ref, o_ref, *, nb, tc):
    j = pl.program_id(1)

    @pl.when(j == 0)
    def _():
        s_ref[...] = s0_ref[...]

    pairs = [(b, p) for b in range(nb) for p in range(N_HEADS // 2)]
    left1 = lax.broadcasted_iota(jnp.int32, (1, 1, LANES), 2) < HEAD_DIM
    left = lax.broadcasted_iota(jnp.int32, (1, HEAD_DIM, LANES), 2) < HEAD_DIM
    eye2 = (lax.broadcasted_iota(jnp.int32, (1, HEAD_DIM, LANES), 1)
            == lax.broadcasted_iota(jnp.int32, (1, HEAD_DIM, LANES), 2) % HEAD_DIM).astype(F32)
    rsum = lambda t: jnp.sum(t, axis=2, keepdims=True)

    def halves(rw):
        return jnp.where(left1, rw, 0.0), jnp.where(left1, 0.0, rw)

    def step(t8, carry):
        base = pl.multiple_of(t8 * SUBLANES, SUBLANES)
        ld = lambda ref: jnp.stack([ref[b, pl.ds(base, SUBLANES), p * LANES:(p + 1) * LANES] for b, p in pairs])
        nkk, rp, v, w, kka, k = (ld(r) for r in (nkk_ref, rp_ref, v_ref, w_ref, kka_ref, k_ref))
        s = s_ref[...].reshape(len(pairs), HEAD_DIM, LANES)
        o_rows = []
        for i in range(SUBLANES):
            row = lambda x: x[:, i:i + 1, :]
            nkk_l, nkk_r = halves(row(nkk))
            rp_l, rp_r = halves(row(rp))
            v_l, v_r = halves(row(v))
            sa = jnp.where(left, rsum(s * nkk_l), rsum(s * nkk_r))
            oc = jnp.where(left, rsum(s * rp_l), rsum(s * rp_r))
            vc = jnp.where(left, rsum(eye2 * v_l), rsum(eye2 * v_r))
            s = s * row(w) + sa * row(kka) + vc * row(k)
            o_rows.append(jnp.sum(oc * eye2, axis=1, keepdims=True))
        s_ref[...] = s.reshape(s_ref.shape)
        o_new = jnp.concatenate(o_rows, axis=1) + ld(ovk_ref)
        for n, (b, p) in enumerate(pairs):
            o_ref[b, pl.ds(base, SUBLANES), p * LANES:(p + 1) * LANES] = o_new[n]
        return carry

    lax.fori_loop(0, tc // SUBLANES, step, 0)

    @pl.when(j == pl.num_programs(1) - 1)
    def _():
        sfin_ref[...] = s_ref[...]

    mean = _seg_ones(GROUP_W, HEAD_DIM) * (1.0 / HEAD_DIM)
    seg_mean = lambda t: jnp.dot(t, mean, precision=HIGHEST, preferred_element_type=F32)
    ln_w, ln_b = vec_ref[5:6, :], vec_ref[6:7, :]
    for b in range(nb):
        o = o_ref[b]
        d = o - seg_mean(o)
        on = d * lax.rsqrt(seg_mean(d * d) + RW_GN_EPS) * ln_w + ln_b
        out_ref[b] = (on + bonus_ref[b]) * g_ref[b]


def _rwkv_mix(p, row0, l, p_prev, s0, lw, nb, tc, tl):
    b = p_prev.shape[0]
    assert l % tc == 0 and b % nb == 0 and tc % SUBLANES == 0 and row0 % tl == 0
    seqs_per_tile = max(tl // l, 1)
    assert (l % tl == 0) if seqs_per_tile == 1 else (tl % l == 0 and b % seqs_per_tile == 0)
    nt = max(l // tl, 1)
    blk0 = row0 // tl
    tile = pl.BlockSpec((tl, GROUP_W), lambda bi, j: (bi * nt + j, 0))
    outs = pl.pallas_call(
        functools.partial(_rw_prep_kernel, seqs=seqs_per_tile), grid=(b // seqs_per_tile, nt),
        in_specs=[pl.BlockSpec((tl, RW_COLS), lambda bi, j: (blk0 + bi * nt + j, 0)),
                  pl.BlockSpec((seqs_per_tile, 1, RW_COLS), lambda bi, j: (bi, 0, 0)),
                  _full((1, RW_COLS)), _full((7, GROUP_W)), _full((W_LORA, GROUP_W)), _full((A_LORA, GROUP_W)),
                  _full((G_LORA, GROUP_W))],
        out_specs=[tile] * 9, out_shape=[jax.ShapeDtypeStruct((b * l, GROUP_W), F32)] * 9,
        scratch_shapes=[pltpu.VMEM((1, RW_COLS), F32)],
        compiler_params=_cparams(("arbitrary", "arbitrary")))(
            p, p_prev[:, None, :], lw['rw_mu'], lw['rw_vec'], lw['rw_w2'], lw['rw_a2'], lw['rw_g2'])
    seqs = [a.reshape(b, l, GROUP_W) for a in outs]
    s0p = s0.reshape(b, 2, 2, HEAD_DIM, HEAD_DIM).transpose(0, 1, 3, 2, 4).reshape(b, 2, HEAD_DIM, LANES)
    seq_spec = pl.BlockSpec((nb, tc, GROUP_W), lambda bi, j: (bi, j, 0))
    st_spec = pl.BlockSpec((nb, 2, HEAD_DIM, LANES), lambda bi, j: (bi, 0, 0, 0))
    out, sfin = pl.pallas_call(
        functools.partial(_rw_scan_kernel, nb=nb, tc=tc), grid=(b // nb, l // tc),
        in_specs=[seq_spec] * 9 + [st_spec, _full((7, GROUP_W))],
        out_specs=[seq_spec, st_spec],
        out_shape=[jax.ShapeDtypeStruct((b, l, GROUP_W), F32), jax.ShapeDtypeStruct(s0p.shape, F32)],
        scratch_shapes=[pltpu.VMEM((nb, 2, HEAD_DIM, LANES), F32), pltpu.VMEM((nb, tc, GROUP_W), F32)],
        compiler_params=_cparams(("arbitrary", "arbitrary")))(*seqs, s0p, lw['rw_vec'])
    sfin = sfin.reshape(b, 2, HEAD_DIM, 2, HEAD_DIM).transpose(0, 1, 3, 2, 4).reshape(b, N_HEADS, HEAD_DIM, HEAD_DIM)
    return out.reshape(b * l, GROUP_W), sfin


HALO = 32


def _conv_kernel(u_ref, buf_ref, dw_ref, vec_ref, pw_ref, y_ref, st_ref, hp_ref, *, tl):
    j = pl.program_id(1)

    @pl.when(j == 0)
    def _():
        hp_ref[0:HALO - (CONV_W - 1), :] = jnp.zeros((HALO - (CONV_W - 1), GROUP_W), F32)
        hp_ref[HALO - (CONV_W - 1):HALO, :] = buf_ref[...]

    @pl.when(j > 0)
    def _():
        hp_ref[0:HALO, :] = hp_ref[tl:tl + HALO, :]

    u = u_ref[...]
    hp_ref[HALO:HALO + tl, :] = u[:, 0:GROUP_W] * jax.nn.sigmoid(u[:, GROUP_W:])
    acc = jnp.zeros((tl, GROUP_W), F32)
    for i in range(CONV_W):
        o = HALO - (CONV_W - 1) + i
        acc = acc + hp_ref[o:o + tl, :] * dw_ref[i:i + 1, :]
    y = acc + vec_ref[0:1, :]
    d = y - jnp.mean(y, axis=-1, keepdims=True)
    y = d * lax.rsqrt(jnp.mean(d * d, axis=-1, keepdims=True) + CONV_LN_EPS) * vec_ref[1:2, :] + vec_ref[2:3, :]
    y_ref[...] = _bdot(y * jax.nn.sigmoid(y), pw_ref[...])

    @pl.when(j == pl.num_programs(1) - 1)
    def _():
        st_ref[...] = hp_ref[tl + HALO - (CONV_W - 1):tl + HALO, :]


def _conv_mix(u, row0, l, buf, lw, tl):
    b = buf.shape[0]
    assert l % tl == 0 and row0 % tl == 0
    nt = l // tl
    blk0 = row0 // tl
    return pl.pallas_call(
        functools.partial(_conv_kernel, tl=tl), grid=(b, nt),
        in_specs=[pl.BlockSpec((tl, 2 * GROUP_W), lambda bi, j: (blk0 + bi * nt + j, 0)),
                  pl.BlockSpec((None, CONV_W - 1, GROUP_W), lambda bi, j: (bi, 0, 0)),
                  _full((CONV_W, GROUP_W)), _full((3, GROUP_W)), _full((GROUP_W, GROUP_W))],
        out_specs=[pl.BlockSpec((tl, GROUP_W), lambda bi, j: (bi * nt + j, 0)),
                   pl.BlockSpec((None, CONV_W - 1, GROUP_W), lambda bi, j: (bi, 0, 0))],
        out_shape=[jax.ShapeDtypeStruct((b * l, GROUP_W), F32), jax.ShapeDtypeStruct(buf.shape, F32)],
        scratch_shapes=[pltpu.VMEM((tl + HALO, GROUP_W), F32)],
        compiler_params=_cparams(("arbitrary", "arbitrary")))(u, buf, lw['conv_dw'], lw['conv_vec'], lw['conv_pw'])


def _mla_prep_kernel(p_ref, cos_ref, sin_ref, qn_ref, kvn_ref, wqn_ref, wqr_ref, wqp_ref, wkb_ref,
                     rows_ref, keys_ref, q_ref):
    p = p_ref[...]
    cos, sin = cos_ref[...], sin_ref[...]
    c_q = _rms(p[:, 0:LANES], qn_ref[...]).astype(BF16)
    c_kv = _rms(p[:, LANES:2 * LANES], kvn_ref[...])
    k_rope = p[:, 2 * LANES:3 * LANES] * cos + p[:, 3 * LANES:4 * LANES] * sin
    rows_ref[:, 0:KV_LORA] = c_kv
    rows_ref[:, KV_LORA:C_MLA] = k_rope[:, 0:QK_ROPE]
    keys_ref[...] = jnp.concatenate([c_kv, k_rope], axis=-1).astype(BF16)
    for h in range(N_HEADS):
        q_nope = jnp.dot(c_q, wqn_ref[h], preferred_element_type=F32)
        q_lat = _bdot(q_nope, wkb_ref[h])
        q_rope = (jnp.dot(c_q, wqr_ref[h], preferred_element_type=F32) * cos
                  + jnp.dot(c_q, wqp_ref[h], preferred_element_type=F32) * sin)
        q_ref[h] = jnp.concatenate([q_lat, q_rope], axis=-1) * MLA_SCALE


def _mla_prep(p_mla, cos, sin, lw, tm):
    m = p_mla.shape[0]
    assert m % tm == 0
    row = lambda w: pl.BlockSpec((tm, w), lambda i: (i, 0))
    return pl.pallas_call(
        _mla_prep_kernel, grid=(m // tm,),
        in_specs=[row(MLA_W), row(LANES), row(LANES), _full((1, LANES)), _full((1, LANES)),
                  _full((N_HEADS, LANES, QK_NOPE)), _full((N_HEADS, LANES, LANES)), _full((N_HEADS, LANES, LANES)),
                  _full((N_HEADS, QK_NOPE, KV_LORA))],
        out_specs=[row(C_MLA), row(2 * LANES), pl.BlockSpec((N_HEADS, tm, 2 * LANES), lambda i: (0, i, 0))],
        out_shape=[jax.ShapeDtypeStruct((m, C_MLA), F32), jax.ShapeDtypeStruct((m, 2 * LANES), BF16),
                   jax.ShapeDtypeStruct((N_HEADS, m, 2 * LANES), F32)],
        compiler_params=_cparams(("parallel",)))(
            p_mla, cos, sin, lw['mla_q_norm'], lw['mla_kv_norm'], lw['mla_wq_nope'], lw['mla_wq_rope'],
            lw['mla_wq_rope_rot'], lw['mla_wkb'])


def _mla_out(o_lat, wvb_ref, rows_per_head):
    out = None
    for h in range(N_HEADS):
        t = _bdot(o_lat[h * rows_per_head:(h + 1) * rows_per_head], wvb_ref[h])
        out = t if out is None else out + t
    return out


def _mla_prompt_kernel(q_ref, keys_ref, wvb_ref, o_ref, *, tq):
    i = pl.program_id(1)
    q = q_ref[...].reshape(N_HEADS * tq, 2 * LANES).astype(BF16)
    t = i * tq + lax.broadcasted_iota(jnp.int32, (N_HEADS * tq, tq), 0) % tq
    col = lax.broadcasted_iota(jnp.int32, (N_HEADS * tq, tq), 1)

    def body(kt, carry, diagonal=False):
        m, l, acc = carry
        k = keys_ref[pl.ds(pl.multiple_of(kt * tq, tq), tq), :]
        s = _bdot_t(q, k)
        if diagonal:
            mask = kt * tq + col <= t
            s = jnp.where(mask, s, NEG)
        m_new = jnp.maximum(m, jnp.max(s, axis=-1, keepdims=True))
        e = jnp.exp(s - m_new)
        if diagonal:
            e = jnp.where(mask, e, 0.0)
        alpha = jnp.exp(m - m_new)
        return m_new, alpha * l + jnp.sum(e, axis=-1, keepdims=True), alpha * acc + _bdot(e, k[:, 0:KV_LORA])

    init = (jnp.full((N_HEADS * tq, 1), NEG, F32), jnp.zeros((N_HEADS * tq, 1), F32),
            jnp.zeros((N_HEADS * tq, KV_LORA), F32))
    m, l, acc = body(i, lax.fori_loop(0, i, body, init), diagonal=True)
    o_ref[...] = _mla_out(acc / l, wvb_ref, tq)


def _mla_prompt(q, keys, wvb, b, l, tq):
    m = b * l
    assert l % tq == 0
    nq = l // tq
    return pl.pallas_call(
        functools.partial(_mla_prompt_kernel, tq=tq), grid=(b, nq),
        in_specs=[pl.BlockSpec((N_HEADS, tq, 2 * LANES), lambda bi, i: (0, bi * nq + i, 0)),
                  pl.BlockSpec((l, 2 * LANES), lambda bi, i: (bi, 0)), _full(wvb.shape)],
        out_specs=pl.BlockSpec((tq, GROUP_W), lambda bi, i: (bi * nq + i, 0)),
        out_shape=jax.ShapeDtypeStruct((m, GROUP_W), F32),
        compiler_params=_cparams(("parallel", "parallel")))(q, keys, wvb)


def _page_copies(table_ref, windows, sem_ref, b, slot, n_pages):
    return [pltpu.make_async_copy(src, dst, sem_ref.at[slot])
            for j in range(n_pages) for src, dst in windows(table_ref[b, j], slot, j)]


def _paged_fetch(table_ref, windows, sem_ref, n_pages):
    b = pl.program_id(0)
    slot = b % 2

    @pl.when(b == 0)
    def _():
        for c in _page_copies(table_ref, windows, sem_ref, b, slot, n_pages):
            c.start()

    @pl.when(b + 1 < pl.num_programs(0))
    def _():
        for c in _page_copies(table_ref, windows, sem_ref, b + 1, 1 - slot, n_pages):
            c.start()

    for c in _page_copies(table_ref, windows, sem_ref, b, slot, n_pages):
        c.wait()
    return slot


def _mla_sample_kernel(table_ref, q_ref, new_ref, pool_ref, wvb_ref, o_ref, buf_ref, sem_ref, *, layer, n_pages, lq):
    page = buf_ref.shape[2] // n_pages
    windows = lambda pid, slot, j: [(pool_ref.at[layer, pid], buf_ref.at[slot, :, pl.ds(j * page, page)])]
    slot = _paged_fetch(table_ref, windows, sem_ref, n_pages)
    past_t = buf_ref[slot].astype(BF16)
    new = new_ref[...].astype(BF16)
    rows = N_HEADS * lq
    q = q_ref[...].reshape(rows, 2 * LANES)[:, 0:C_MLA].astype(BF16)
    s_past = jnp.dot(q, past_t, preferred_element_type=F32)
    qi = lax.broadcasted_iota(jnp.int32, (rows, lq), 0) % lq
    mask = lax.broadcasted_iota(jnp.int32, (rows, lq), 1) <= qi
    s_new = jnp.where(mask, _bdot_t(q, new), NEG)
    m = jnp.maximum(jnp.max(s_past, axis=-1, keepdims=True), jnp.max(s_new, axis=-1, keepdims=True))
    e_past = jnp.exp(s_past - m)
    e_new = jnp.where(mask, jnp.exp(s_new - m), 0.0)
    den = jnp.sum(e_past, axis=-1, keepdims=True) + jnp.sum(e_new, axis=-1, keepdims=True)
    o_lat = (_bdot_t(e_past, past_t[0:KV_LORA, :]) + _bdot(e_new, new[:, 0:KV_LORA])) / den
    o_ref[...] = _mla_out(o_lat, wvb_ref, lq)


def _mla_sample(q, new_rows, pool_t, page_table, wvb, layer, row0):
    b, n_pages = page_table.shape
    page = pool_t.shape[3]
    lq = (q.shape[1] - row0) // b
    assert row0 % lq == 0 and page % LANES == 0
    blk0 = row0 // lq
    grid_spec = pltpu.PrefetchScalarGridSpec(
        num_scalar_prefetch=1, grid=(b,),
        in_specs=[pl.BlockSpec((N_HEADS, lq, 2 * LANES), lambda i, tbl: (0, blk0 + i, 0)),
                  pl.BlockSpec((lq, C_MLA), lambda i, tbl: (blk0 + i, 0)),
                  pl.BlockSpec(memory_space=pl.ANY),
                  pl.BlockSpec(wvb.shape, lambda i, tbl: (0, 0, 0))],
        out_specs=pl.BlockSpec((lq, GROUP_W), lambda i, tbl: (i, 0)),
        scratch_shapes=[pltpu.VMEM((2, C_MLA, n_pages * page), F32), pltpu.SemaphoreType.DMA((2,))])
    return pl.pallas_call(
        functools.partial(_mla_sample_kernel, layer=layer, n_pages=n_pages, lq=lq), grid_spec=grid_spec,
        out_shape=jax.ShapeDtypeStruct((b * lq, GROUP_W), F32),
        compiler_params=_cparams(("arbitrary",)))(page_table, q, new_rows, pool_t, wvb)


def _gelu_tanh(x):
    return 0.5 * x * (1.0 + jnp.tanh(0.7978845608028654 * (x + 0.044715 * x * x * x)))


def _compress(a, wexp_ref, pos_ref, b1_ref, w2k_ref, w2v_ref, b2_ref):
    n_ch = a.shape[0]
    c = jnp.dot(pos_ref[...], wexp_ref[...], preferred_element_type=F32)
    up = lambda t: pltpu.roll(t, n_ch - 1, axis=0)
    hk = a[:, 0:LANES] + up(a[:, LANES:2 * LANES]) + c[0:1, 0:LANES] + c[1:2, LANES:2 * LANES] + b1_ref[:, 0:LANES]
    hv = (a[:, 2 * LANES:3 * LANES] + up(a[:, 3 * LANES:4 * LANES]) + c[0:1, 2 * LANES:3 * LANES]
          + c[1:2, 3 * LANES:4 * LANES] + b1_ref[:, LANES:2 * LANES])
    out = _bdot(_gelu_tanh(hk), w2k_ref[...]) + _bdot(_gelu_tanh(hv), w2v_ref[...]) + b2_ref[...]
    valid = lax.broadcasted_iota(jnp.int32, out.shape, 0) < n_ch - 1
    return jnp.where(valid, out, 0.0)


def _compress_kernel(ch_ref, wexp_ref, pos_ref, b1_ref, w2k_ref, w2v_ref, b2_ref, o_ref):
    a = jnp.dot(ch_ref[...].astype(BF16), wexp_ref[...], preferred_element_type=F32)
    o_ref[...] = _compress(a, wexp_ref, pos_ref, b1_ref, w2k_ref, w2v_ref, b2_ref)


def _stack_heads(q, scale):
    low = lax.broadcasted_iota(jnp.int32, (q.shape[0], LANES), 1) < HEAD_DIM
    parts = []
    for h in range(N_HEADS):
        slab = q[:, (h // 2) * LANES:(h // 2 + 1) * LANES]
        if h % 2:
            slab = pltpu.roll(slab, HEAD_DIM, axis=1)
        parts.append(jnp.where(low, slab * scale, 0.0))
    return jnp.concatenate(parts, axis=0).astype(BF16)


def _softmax_parts(parts):
    m = None
    for s, mask in parts:
        mx = jnp.max(jnp.where(mask, s, NEG), axis=-1, keepdims=True)
        m = mx if m is None else jnp.maximum(m, mx)
    es = [jnp.where(mask, jnp.exp(jnp.where(mask, s, NEG) - m), 0.0) for s, mask in parts]
    den = sum(jnp.sum(e, axis=-1, keepdims=True) for e in es)
    return es, jnp.where(den > 0.0, den, 1.0)


def _select_blocks(p_c, t, n_cb, n_sb, nq, fillers=()):
    ncbp = p_c.shape[1]
    nsbp = -(-n_sb // LANES) * LANES
    psum = p_c[0:nq] + p_c[nq:2 * nq] + p_c[2 * nq:3 * nq] + p_c[3 * nq:4 * nq]
    ci = lax.broadcasted_iota(jnp.int32, (ncbp, nsbp), 0)
    sj = lax.broadcasted_iota(jnp.int32, (ncbp, nsbp), 1)
    overlap = ((CMP_STRIDE * ci < SLC_BLOCK * (sj + 1)) & (CMP_STRIDE * ci + CMP_LEN > SLC_BLOCK * sj)
               & (ci < n_cb) & (sj < n_sb)).astype(BF16)
    hi = psum.astype(BF16)
    lo = (psum - hi.astype(F32)).astype(BF16)
    imp = jnp.dot(hi, overlap, preferred_element_type=F32) + jnp.dot(lo, overlap, preferred_element_type=F32)
    j = lax.broadcasted_iota(jnp.int32, (nq, nsbp), 1)
    cur = t // SLC_BLOCK
    forced = (j == 0) | (j == cur) | (j == cur - 1)
    score = jnp.where(forced, FORCE_SCORE, jnp.where(j <= cur, imp, -1.0))
    score = jnp.where(j < n_sb, score, -3e38)
    sel = jnp.zeros((nq, nsbp), F32)
    fillers = list(fillers)
    filled = []
    rounds = min(N_SELECT, n_sb)
    for r in range(rounds):
        mx = jnp.max(score, axis=-1, keepdims=True)
        first = jnp.min(jnp.where(score == mx, j, nsbp), axis=-1, keepdims=True)
        hit = j == first
        sel = jnp.where(hit, 1.0, sel)
        score = jnp.where(hit, -3e38, score)
        share = -(-len(fillers) // (rounds - r))
        filled += [f() for f in fillers[:share]]
        fillers = fillers[share:]
    return sel.astype(BF16), filled


def _tile4(x):
    return jnp.concatenate([x] * N_HEADS, axis=0)


def _gate_combine(gates, o_c, o_s, o_w, o_ref, nq):
    for h in range(N_HEADS):
        rows = slice(h * nq, (h + 1) * nq)
        g = lambda c: gates[:, 3 * h + c:3 * h + c + 1]
        o = g(0) * o_c[rows] + g(1) * o_s[rows] + g(2) * o_w[rows]
        o_ref[:, h * HEAD_DIM:(h + 1) * HEAD_DIM] = o[:, HEAD_DIM:LANES]


def _nsa_prompt_kernel(q_ref, g_ref, cmp_ref, slc_ref, win_ref, o_ref, slc_s, win_s, *, tq, l, tks):
    i = pl.program_id(1)

    @pl.when(i == 0)
    def _():
        slc_s[...] = slc_ref[...].astype(BF16)
        win_s[...] = win_ref[...].astype(BF16)

    rows = N_HEADS * tq
    n_cb, n_sb = l // CMP_STRIDE - 1, l // SLC_BLOCK
    q = _stack_heads(q_ref[...], HEAD_DIM ** -0.5)
    tq1 = i * tq + lax.broadcasted_iota(jnp.int32, (tq, 1), 0)
    t = _tile4(tq1)
    kcvc = cmp_ref[...].astype(BF16)
    n = lax.broadcasted_iota(jnp.int32, (rows, kcvc.shape[0]), 1)
    (e_c,), den = _softmax_parts([(_bdot_t(q, kcvc), (CMP_STRIDE * n + CMP_LEN <= t + 1) & (n < n_cb))])
    p_c = e_c / den
    span = WINDOW + tq
    w0 = pl.multiple_of(jnp.maximum(i * tq - WINDOW, 0), tq)
    wpos = w0 + lax.broadcasted_iota(jnp.int32, (tq, span), 1)
    wmask = (wpos <= tq1) & (wpos > tq1 - WINDOW)

    def window_head(h):
        kvw = win_s[pl.ds(w0, span), :]
        (e_w,), den_w = _softmax_parts([(_bdot_t(q[h * tq:(h + 1) * tq], kvw), wmask)])
        return _bdot(e_w, kvw) / den_w

    fillers = [functools.partial(window_head, h) for h in range(N_HEADS)] + [lambda: _bdot(p_c, kcvc)]
    sel, (*o_w, o_c) = _select_blocks(p_c, tq1, n_cb, n_sb, tq, fillers)
    o_w = jnp.concatenate(o_w, axis=0)
    blk = lax.broadcasted_iota(jnp.int32, (sel.shape[1], tks), 0)
    kcol = lax.broadcasted_iota(jnp.int32, (sel.shape[1], tks), 1)
    col = lax.broadcasted_iota(jnp.int32, (rows, tks), 1)

    def body(kt, carry):
        m, lsum, acc = carry
        k0 = pl.multiple_of(kt * tks, tks)
        kv = slc_s[pl.ds(k0, tks), :]
        expand = (blk == (k0 + kcol) // SLC_BLOCK).astype(BF16)
        chosen = _tile4(jnp.dot(sel, expand, preferred_element_type=F32)) > 0.5
        mask = chosen & (k0 + col <= t)
        s = jnp.where(mask, _bdot_t(q, kv), NEG)
        m_new = jnp.maximum(m, jnp.max(s, axis=-1, keepdims=True))
        e = jnp.where(mask, jnp.exp(s - m_new), 0.0)
        alpha = jnp.exp(m - m_new)
        return m_new, alpha * lsum + jnp.sum(e, axis=-1, keepdims=True), alpha * acc + _bdot(e, kv)

    init = (jnp.full((rows, 1), NEG, F32), jnp.zeros((rows, 1), F32), jnp.zeros((rows, LANES), F32))
    _, lsum, acc = lax.fori_loop(0, ((i + 1) * tq + tks - 1) // tks, body, init)
    o_s = acc / jnp.where(lsum > 0.0, lsum, 1.0)
    _gate_combine(jax.nn.sigmoid(g_ref[...]), o_c, o_s, o_w, o_ref, tq)


def _nsa_prompt(p_nsa, lw, b, l, tq):
    m = b * l
    n_ch = l // CMP_STRIDE
    tks = 4 * SLC_BLOCK
    assert l % tq == 0 and l >= WINDOW + tq and l % tks == 0 and tq % SLC_BLOCK == 0 and WINDOW % tq == 0
    ch = p_nsa[:m, GROUP_W:GROUP_W + LANES].reshape(m // CMP_STRIDE, CMP_STRIDE * LANES)
    kcvc = pl.pallas_call(
        _compress_kernel, grid=(b,),
        in_specs=[pl.BlockSpec((n_ch, CMP_STRIDE * LANES), lambda bi: (bi, 0)), _full(lw['cmp_wexp_p'].shape),
                  _full(lw['cmp_pos_p'].shape), _full((1, 2 * LANES)), _full((LANES, LANES)), _full((LANES, LANES)),
                  _full((1, LANES))],
        out_specs=pl.BlockSpec((n_ch, LANES), lambda bi: (bi, 0)),
        out_shape=jax.ShapeDtypeStruct((b * n_ch, LANES), F32),
        compiler_params=_cparams(("parallel",)))(
            ch, lw['cmp_wexp_p'], lw['cmp_pos_p'], lw['cmp_b1'], lw['cmp_w2k'], lw['cmp_w2v'], lw['cmp_b2'])
    nq = l // tq
    slab = lambda c: pl.BlockSpec((l, LANES), lambda bi, i: (bi, c))
    return pl.pallas_call(
        functools.partial(_nsa_prompt_kernel, tq=tq, l=l, tks=tks), grid=(b, nq),
        in_specs=[pl.BlockSpec((tq, GROUP_W), lambda bi, i: (bi * nq + i, 0)),
                  pl.BlockSpec((tq, LANES), lambda bi, i: (bi * nq + i, NSA_W // LANES - 1)),
                  pl.BlockSpec((n_ch, LANES), lambda bi, i: (bi, 0)), slab(3), slab(4)],
        out_specs=pl.BlockSpec((tq, GROUP_W), lambda bi, i: (bi * nq + i, 0)),
        out_shape=jax.ShapeDtypeStruct((m, GROUP_W), F32),
        scratch_shapes=[pltpu.VMEM((l, LANES), BF16), pltpu.VMEM((l, LANES), BF16)],
        compiler_params=_cparams(("parallel", "arbitrary")))(p_nsa, p_nsa, kcvc, p_nsa, p_nsa)


def _nsa_sample_kernel(table_ref, q_ref, g_ref, nslc_ref, nwin_ref, win_ref, pool_ref, wexp_ref, pos_ref, b1_ref,
                       w2k_ref, w2v_ref, b2_ref, o_ref, wout_ref, cmp_buf, slc_buf, sem_ref, *, layer, n_pages, lq, past,
                       tks):
    page = past // n_pages

    def windows(pid, slot, j):
        return [(pool_ref.at[layer, pid, :, pl.ds(c * LANES, LANES)], buf.at[slot, pl.ds(j * page, page), :])
                for c, buf in enumerate((cmp_buf, slc_buf))]

    slot = _paged_fetch(table_ref, windows, sem_ref, n_pages)
    n_ch = past // CMP_STRIDE
    rows = N_HEADS * lq
    n_cb = (past + lq) // CMP_STRIDE - 1
    n_sb = -(-(past + lq) // SLC_BLOCK)
    q = _stack_heads(q_ref[...], HEAD_DIM ** -0.5)
    tq1 = past + lax.broadcasted_iota(jnp.int32, (lq, 1), 0)
    t = _tile4(tq1)
    a = jnp.zeros((n_ch, 4 * CMP_HID), F32)
    for r in range(0, CMP_STRIDE, 2):
        x = jnp.concatenate([cmp_buf[slot, pl.ds(r + d, n_ch, stride=CMP_STRIDE), :].astype(BF16) for d in range(2)],
                            axis=1)
        a = a + jnp.dot(x, wexp_ref[r * LANES:(r + 2) * LANES, :], preferred_element_type=F32)
    kcvc = _compress(a, wexp_ref, pos_ref, b1_ref, w2k_ref, w2v_ref, b2_ref).astype(BF16)
    n = lax.broadcasted_iota(jnp.int32, (rows, n_ch), 1)
    (e_c,), den = _softmax_parts([(_bdot_t(q, kcvc), (CMP_STRIDE * n + CMP_LEN <= t + 1) & (n < n_cb))])
    p_c = e_c / den
    qi = lax.broadcasted_iota(jnp.int32, (rows, lq), 0) % lq
    causal_new = lax.broadcasted_iota(jnp.int32, (rows, lq), 1) <= qi

    def window():
        win = win_ref[...]
        nwin = nwin_ref[...]
        wb = win.shape[0]
        wpos = past - wb + lax.broadcasted_iota(jnp.int32, (rows, wb), 1)
        es, den_w = _softmax_parts([(_bdot_t(q, win), (wpos <= t) & (wpos > t - WINDOW) & (wpos >= 0)),
                                    (_bdot_t(q, nwin), causal_new)])
        wout_ref[0:wb - lq, :] = win[lq:wb, :]
        wout_ref[wb - lq:wb, :] = nwin
        return (_bdot(es[0], win) + _bdot(es[1], nwin)) / den_w

    def slc_scores(kt):
        kv = slc_buf[slot, pl.ds(kt * tks, tks), :].astype(BF16)
        return kv, _bdot_t(q, kv)

    fillers = ([window, lambda: _bdot(p_c, kcvc)] + [functools.partial(slc_scores, kt) for kt in range(past // tks)])
    sel, (o_w, o_c, *tiles) = _select_blocks(p_c, tq1, n_cb, n_sb, lq, fillers)
    nsbp = sel.shape[1]
    blk = lax.broadcasted_iota(jnp.int32, (nsbp, tks), 0)
    kcol = lax.broadcasted_iota(jnp.int32, (nsbp, tks), 1)
    parts = []
    for kt, (kv, s) in enumerate(tiles):
        expand = (blk == (kt * tks + kcol) // SLC_BLOCK).astype(BF16)
        parts.append((s, _tile4(jnp.dot(sel, expand, preferred_element_type=F32)) > 0.5))
    nslc = nslc_ref[...].astype(BF16)
    new_blk = past // SLC_BLOCK
    sel_new = _tile4(sel[:, new_blk:new_blk + 1].astype(F32)) > 0.5
    parts.append((_bdot_t(q, nslc), causal_new & sel_new))
    es, den = _softmax_parts(parts)
    acc = _bdot(es[-1], nslc)
    for e, (kv, _) in zip(es[:-1], tiles):
        acc = acc + _bdot(e, kv)
    o_s = acc / den
    _gate_combine(jax.nn.sigmoid(g_ref[...]), o_c, o_s, o_w, o_ref, lq)


def _nsa_sample(p_nsa, row0, pool, page_table, win_cache, lw, layer):
    b, n_pages = page_table.shape
    page, c_in = pool.shape[2:]
    past = n_pages * page
    lq = (p_nsa.shape[0] - row0) // b
    wb = win_cache.shape[1]
    tks = min(past, 1024)
    assert row0 % lq == 0 and page % CMP_STRIDE == 0 and lq < CMP_STRIDE and lq % 8 == 0 and past % tks == 0
    blk0 = row0 // lq
    row = lambda w, c: pl.BlockSpec((lq, w), lambda i, tbl: (blk0 + i, c))
    const = lambda a: pl.BlockSpec(a.shape, lambda i, tbl: (0,) * a.ndim)
    weights = [lw['cmp_wexp_p'], lw['cmp_pos_p'], lw['cmp_b1'], lw['cmp_w2k'], lw['cmp_w2v'], lw['cmp_b2']]
    grid_spec = pltpu.PrefetchScalarGridSpec(
        num_scalar_prefetch=1, grid=(b,),
        in_specs=[row(GROUP_W, 0), row(LANES, NSA_W // LANES - 1), row(LANES, 3), row(LANES, 4),
                  pl.BlockSpec((None, wb, LANES), lambda i, tbl: (i, 0, 0)),
                  pl.BlockSpec(memory_space=pl.ANY)] + [const(w) for w in weights],
        out_specs=[pl.BlockSpec((lq, GROUP_W), lambda i, tbl: (i, 0)),
                   pl.BlockSpec((None, wb, LANES), lambda i, tbl: (i, 0, 0))],
        scratch_shapes=[pltpu.VMEM((2, past, LANES), F32), pltpu.VMEM((2, past, LANES), F32),
                        pltpu.SemaphoreType.DMA((2,))])
    return pl.pallas_call(
        functools.partial(_nsa_sample_kernel, layer=layer, n_pages=n_pages, lq=lq, past=past, tks=tks),
        grid_spec=grid_spec,
        out_shape=[jax.ShapeDtypeStruct((b * lq, GROUP_W), F32), jax.ShapeDtypeStruct(win_cache.shape, F32)],
        compiler_params=_cparams(("arbitrary",)))(page_table, p_nsa, p_nsa, p_nsa, p_nsa, win_cache, pool, *weights)


def _pad_cols(w, width):
    return jnp.pad(w, ((0, 0), (0, width - w.shape[1])))


def _rot_cols(w):
    half = QK_ROPE // 2
    return jnp.concatenate([-w[:, half:], w[:, :half]], axis=1)


def _expand_cmp_w1(w1, channels):
    half = CMP_STRIDE * HEAD_DIM
    out = jnp.zeros((CMP_STRIDE, channels, 4 * CMP_HID), F32)
    for br in range(2):
        for part in range(2):
            blk = w1[br, part * half:(part + 1) * half].reshape(CMP_STRIDE, HEAD_DIM, CMP_HID)
            c0 = (2 * br + part) * CMP_HID
            out = out.at[:, br * HEAD_DIM:(br + 1) * HEAD_DIM, c0:c0 + CMP_HID].set(blk)
    return out.reshape(CMP_STRIDE * channels, 4 * CMP_HID).astype(BF16)


def _expand_cmp_pos(pos, channels):
    out = jnp.zeros((8, CMP_STRIDE, channels), F32)
    for br in range(2):
        for part in range(2):
            out = out.at[part, :, br * HEAD_DIM:(br + 1) * HEAD_DIM].set(
                pos[br, part * CMP_STRIDE:(part + 1) * CMP_STRIDE])
    return out.reshape(8, CMP_STRIDE * channels).astype(BF16)


def _layer_weights(w, l):
    nsa0 = RW_COLS
    mla0 = nsa0 + GROUP_W + 6 * HEAD_DIM + 3 * N_HEADS
    conv0 = mla0 + 2 * LANES + QK_ROPE
    w_in = w['w_in'][l]
    k_rope = w_in[:, mla0 + 2 * LANES:conv0]
    w_in_ext = jnp.concatenate([
        w_in[:, :nsa0], _pad_cols(w_in[:, nsa0:mla0], NSA_W), w_in[:, mla0:mla0 + 2 * LANES],
        _pad_cols(k_rope, LANES), _pad_cols(_rot_cols(k_rope), LANES), w_in[:, conv0:]], axis=1).astype(BF16)
    d_hd = QK_NOPE + QK_ROPE
    wqb = w['mla_w_qb'][l].reshape(-1, N_HEADS, d_hd).transpose(1, 0, 2)
    wq_rope = wqb[:, :, QK_NOPE:]
    pad3 = lambda a: jnp.pad(a, ((0, 0), (0, 0), (0, LANES - a.shape[2]))).astype(BF16)
    wvb = jnp.zeros((N_HEADS, KV_LORA, GROUP_W), F32)
    for h in range(N_HEADS):
        wvb = wvb.at[h, :, h * HEAD_DIM:(h + 1) * HEAD_DIM].set(w['mla_w_vb'][l][:, h, :])
    w2 = w['cmp_w2'][l]
    return dict(
        norms=w['norms'][l][:, None, :],
        ffn_wg=w['ffn_w_gate'][l].astype(BF16), ffn_wu=w['ffn_w_up'][l].astype(BF16),
        ffn_wd=w['ffn_w_down'][l].astype(BF16),
        w_in_ext=w_in_ext, w_out=w['w_out'][l].astype(BF16),
        rw_mu=w['rw_mu'][l][None, :], rw_vec=w['rw_vec'][l], rw_w2=w['rw_w2'][l].astype(BF16),
        rw_a2=w['rw_a2'][l].astype(BF16), rw_g2=w['rw_g2'][l].astype(BF16),
        cmp_wexp_p=_expand_cmp_w1(w['cmp_w1'][l], LANES), cmp_pos_p=_expand_cmp_pos(w['cmp_pos'][l], LANES),
        cmp_b1=w['cmp_b1'][l].reshape(1, 2 * CMP_HID),
        cmp_w2k=_pad_cols(w2[0], LANES).astype(BF16),
        cmp_w2v=jnp.pad(w2[1], ((0, 0), (HEAD_DIM, 0))).astype(BF16),
        cmp_b2=w['cmp_b2'][l].reshape(1, 2 * HEAD_DIM),
        mla_q_norm=w['mla_q_norm'][l][None, :], mla_kv_norm=w['mla_kv_norm'][l][None, :],
        mla_wq_nope=wqb[:, :, :QK_NOPE].astype(BF16), mla_wq_rope=pad3(wq_rope),
        mla_wq_rope_rot=pad3(jnp.concatenate([-wq_rope[:, :, QK_ROPE // 2:], wq_rope[:, :, :QK_ROPE // 2]], axis=2)),
        mla_wkb=w['mla_w_kb'][l].transpose(1, 2, 0).astype(BF16), mla_wvb=wvb.astype(BF16),
        conv_dw=w['conv_dw'][l], conv_vec=w['conv_vec'][l], conv_pw=w['conv_pw'][l].astype(BF16),
        x_wq=w['x_wq'][l].astype(BF16), x_wkv=w['x_wkv'][l].astype(BF16), x_wo=w['x_wo'][l].astype(BF16))


def _rope_tables(pos):
    half = QK_ROPE // 2
    inv = ROPE_THETA ** (-jnp.arange(half, dtype=F32) / half)
    ang = pos.astype(F32)[:, None] * inv
    tile = lambda a: jnp.tile(a, (1, LANES // half))
    return tile(jnp.cos(ang)), tile(jnp.sin(ang))


def _pick(n, *cands):
    for c in cands:
        if n % c == 0:
            return c
    return n


def kernel(x_prompt, x_sample, cache_mla, cache_nsa, cache_nsa_win, cache_mem, state_rwkv, state_rwkv_shift,
           state_conv, page_table, mem_prompt, norms, ffn_w_gate, ffn_w_up, ffn_w_down, w_in, w_out, rw_mu, rw_vec,
           rw_w2, rw_a2, rw_g2, cmp_pos, cmp_w1, cmp_b1, cmp_w2, cmp_b2, mla_q_norm, mla_kv_norm, mla_w_qb, mla_w_kb,
           mla_w_vb, conv_dw, conv_vec, conv_pw, x_wq, x_wkv, x_wo, final_norm):
    w = dict(norms=norms, ffn_w_gate=ffn_w_gate, ffn_w_up=ffn_w_up, ffn_w_down=ffn_w_down, w_in=w_in, w_out=w_out,
             rw_mu=rw_mu, rw_vec=rw_vec, rw_w2=rw_w2, rw_a2=rw_a2, rw_g2=rw_g2, cmp_pos=cmp_pos, cmp_w1=cmp_w1,
             cmp_b1=cmp_b1, cmp_w2=cmp_w2, cmp_b2=cmp_b2, mla_q_norm=mla_q_norm, mla_kv_norm=mla_kv_norm,
             mla_w_qb=mla_w_qb, mla_w_kb=mla_w_kb, mla_w_vb=mla_w_vb, conv_dw=conv_dw, conv_vec=conv_vec,
             conv_pw=conv_pw, x_wq=x_wq, x_wkv=x_wkv, x_wo=x_wo)
    depth = norms.shape[0]
    bp, lp, d = x_prompt.shape
    bs, ls, _ = x_sample.shape
    mp, ms = bp * lp, bs * ls
    m = mp + ms
    n_pages, page = page_table.shape[1], cache_mla.shape[2]
    past = n_pages * page
    d_ff = ffn_w_gate.shape[-1]
    chunk = _ff_chunk(d_ff)
    tm = _pick(m, 512, 256, 128, 64, 8)
    n_mem = mem_prompt.shape[1]

    x = jnp.concatenate([x_prompt.reshape(mp, d), x_sample.reshape(ms, d)], axis=0)
    cos_p, sin_p = _rope_tables(jnp.arange(lp, dtype=jnp.int32))
    cos_s, sin_s = _rope_tables(past + jnp.arange(ls, dtype=jnp.int32))
    cos = jnp.concatenate([jnp.tile(cos_p, (bp, 1)), jnp.tile(cos_s, (bs, 1))], axis=0)
    sin = jnp.concatenate([jnp.tile(sin_p, (bp, 1)), jnp.tile(sin_s, (bs, 1))], axis=0)
    mem_rows = mem_prompt.reshape(bp * n_mem, d)
    cache_mla_t = jnp.swapaxes(cache_mla, 2, 3)
    zeros = lambda *s: jnp.zeros(s, F32)
    sp, ss = [], []
    both = lambda a, c: jnp.concatenate([a, c], axis=0)

    for l in range(depth):
        lw = _layer_weights(w, l)
        n = lw['norms']
        ffn_w = lambda i: (lw['ffn_wg'][i], lw['ffn_wu'][i])
        (h1,) = _token_call(functools.partial(_ffn_a_kernel, chunk=chunk), [x], [n[0], *ffn_w(0)], [d_ff], [BF16], tm)
        x1, p_rw, p_nsa, p_mla, p_conv = _token_call(
            _ffn_b_proj_kernel, [x, h1], [lw['ffn_wd'][0], n[1], lw['w_in_ext']],
            [d, RW_COLS, NSA_W, MLA_W, 2 * GROUP_W], [F32] * 5, tm)
        (mem_kv,) = _token_call(_norm_proj_kernel, [mem_rows], [n[3], lw['x_wkv']], [2 * GROUP_W], [F32],
                                _pick(bp * n_mem, 256, 8))
        mem_kv = mem_kv.reshape(bp, n_mem, 2 * GROUP_W)
        o_rw_p, rw_state_p = _rwkv_mix(p_rw, 0, lp, zeros(bp, RW_COLS), zeros(bp, N_HEADS, HEAD_DIM, HEAD_DIM), lw,
                                       nb=bp, tc=_pick(lp, 256, 8), tl=_pick(lp, 512, 8))
        o_rw_s, rw_state_s = _rwkv_mix(p_rw, mp, ls, state_rwkv_shift[l], state_rwkv[l], lw,
                                       nb=_pick(bs, 4, 1), tc=ls, tl=_pick(ms, 512, 64, ls))
        o_nsa_p = _nsa_prompt(p_nsa, lw, bp, lp, tq=_pick(lp, 256, 128))
        o_nsa_s, win_s = _nsa_sample(p_nsa, mp, cache_nsa, page_table, cache_nsa_win[l], lw, l)
        mla_rows, mla_keys, mla_q = _mla_prep(p_mla, cos, sin, lw, tm)
        o_mla_p = _mla_prompt(mla_q, mla_keys, lw['mla_wvb'], bp, lp, tq=_pick(lp, 256, 8))
        o_mla_s = _mla_sample(mla_q, mla_rows, cache_mla_t, page_table, lw['mla_wvb'], l, mp)
        o_conv_p, conv_p = _conv_mix(p_conv, 0, lp, zeros(bp, CONV_W - 1, GROUP_W), lw, tl=_pick(lp, 512, 8))
        o_conv_s, conv_s = _conv_mix(p_conv, mp, ls, state_conv[l], lw, tl=ls)
        x2, qx = _token_call(
            _mix_out_kernel,
            [x1, both(o_rw_p, o_rw_s), both(o_nsa_p, o_nsa_s), both(o_mla_p, o_mla_s), both(o_conv_p, o_conv_s)],
            [lw['w_out'], n[2], lw['x_wq']], [d, GROUP_W], [F32] * 2, tm)
        ox = both(_cross_attend(qx, 0, lp, mem_kv, 1, _pick(lp, 512, 8)),
                  _cross_attend(qx, mp, ls, cache_mem[l], _pick(bs, 8, 1), ls))
        x3, h2 = _token_call(functools.partial(_xo_ffn_a_kernel, chunk=chunk), [x2, ox],
                             [lw['x_wo'], n[4], *ffn_w(1)], [d, d_ff], [F32, BF16], tm)
        (x,) = _token_call(_ffn_b_kernel, [x3, h2], [lw['ffn_wd'][1]], [d], [F32], tm)

        nsa_rows = p_nsa[:, GROUP_W:GROUP_W + 4 * HEAD_DIM]
        keep = min(WINDOW, lp)
        nsa_win_p = p_nsa[:mp, GROUP_W + 4 * HEAD_DIM:GROUP_W + 6 * HEAD_DIM].reshape(bp, lp, 2 * HEAD_DIM)[:, lp - keep:]
        sp.append((mla_rows[:mp].reshape(bp, lp, C_MLA), nsa_rows[:mp].reshape(bp, lp, 4 * HEAD_DIM), nsa_win_p,
                   mem_kv, rw_state_p, p_rw[:mp].reshape(bp, lp, RW_COLS)[:, -1], conv_p))
        ss.append((mla_rows[mp:].reshape(bs, ls, C_MLA), nsa_rows[mp:].reshape(bs, ls, 4 * HEAD_DIM), win_s,
                   rw_state_s, p_rw[mp:].reshape(bs, ls, RW_COLS)[:, -1], conv_s))

    (y,) = _token_call(_norm_kernel, [x], [final_norm[None, :]], [d], [F32], tm)
    stack = lambda states, i: jnp.stack([s[i] for s in states])
    return (y[:mp].reshape(bp, lp, d), y[mp:].reshape(bs, ls, d),
            *(stack(sp, i) for i in range(7)), *(stack(ss, i) for i in range(6)))
```

```python
import functools

import jax
import jax.numpy as jnp
from jax import lax
from jax.experimental import pallas as pl
from jax.experimental.pallas import tpu as pltpu

F32 = jnp.float32
BF16 = jnp.bfloat16
HIGHEST = lax.Precision.HIGHEST

GROUP_W = 256
HEAD_DIM = 64
N_HEADS = 4
LANES = 128
SUBLANES = 8
W_LORA, A_LORA, G_LORA = 32, 32, 64
RW_COLS = 3 * GROUP_W + W_LORA + A_LORA + G_LORA
RW_GN_EPS = 64e-5
CMP_LEN, CMP_STRIDE, CMP_HID = 32, 16, 128
SLC_BLOCK, N_SELECT, WINDOW = 64, 16, 512
KV_LORA, QK_NOPE, QK_ROPE = 128, 64, 32
C_MLA = KV_LORA + QK_ROPE
MLA_SCALE = (QK_NOPE + QK_ROPE) ** -0.5
ROPE_THETA = 10000.0
CONV_W = 31
CONV_LN_EPS = 1e-5
NORM_EPS = 1e-6
NEG = -1e30
FORCE_SCORE = 1e4
NSA_W = 768
MLA_W = 512
VMEM_LIMIT = 56 * 1024 * 1024


def _cparams(sem, vmem=VMEM_LIMIT):
    return pltpu.CompilerParams(dimension_semantics=sem, vmem_limit_bytes=vmem)


def _rms(x, g):
    return x * lax.rsqrt(jnp.mean(x * x, axis=-1, keepdims=True) + NORM_EPS) * g


def _bdot(a, b):
    return jnp.dot(a.astype(BF16), b.astype(BF16), preferred_element_type=F32)


def _bdot_t(a, b):
    return lax.dot_general(a.astype(BF16), b.astype(BF16), (((1,), (1,)), ((), ())), preferred_element_type=F32)


def _seg_ones(n, seg):
    r = lax.broadcasted_iota(jnp.int32, (n, n), 0) // seg
    c = lax.broadcasted_iota(jnp.int32, (n, n), 1) // seg
    return (r == c).astype(F32)


def _full(shape):
    nd = len(shape)
    return pl.BlockSpec(shape, lambda *_: (0,) * nd)


def _swiglu_to(xn, wg_ref, wu_ref, h_ref, chunk):
    for c in range(h_ref.shape[1] // chunk):
        sl = slice(c * chunk, (c + 1) * chunk)
        gate = jnp.dot(xn, wg_ref[:, sl], preferred_element_type=F32)
        up = jnp.dot(xn, wu_ref[:, sl], preferred_element_type=F32)
        h_ref[:, sl] = (gate * jax.nn.sigmoid(gate) * up).astype(BF16)


def _ffn_a_kernel(x_ref, g_ref, wg_ref, wu_ref, h_ref, *, chunk):
    xn = _rms(x_ref[...], g_ref[...]).astype(BF16)
    _swiglu_to(xn, wg_ref, wu_ref, h_ref, chunk)


def _ffn_b_proj_kernel(x_ref, h_ref, wd_ref, g_ref, win_ref, x1_ref, prw_ref, pnsa_ref, pmla_ref, pconv_ref):
    x1 = x_ref[...] + 0.5 * jnp.dot(h_ref[...], wd_ref[...], preferred_element_type=F32)
    x1_ref[...] = x1
    xn = _rms(x1, g_ref[...]).astype(BF16)
    off = 0
    for ref in (prw_ref, pnsa_ref, pmla_ref, pconv_ref):
        w = ref.shape[1]
        ref[...] = jnp.dot(xn, win_ref[:, off:off + w], preferred_element_type=F32)
        off += w


def _mix_out_kernel(x_ref, orw_ref, onsa_ref, omla_ref, oconv_ref, wo_ref, g_ref, wq_ref, x2_ref, q_ref):
    acc = x_ref[...]
    for i, ref in enumerate((orw_ref, onsa_ref, omla_ref, oconv_ref)):
        acc = acc + _bdot(ref[...], wo_ref[i * GROUP_W:(i + 1) * GROUP_W, :])
    x2_ref[...] = acc
    q_ref[...] = _bdot(_rms(acc, g_ref[...]), wq_ref[...]) * (HEAD_DIM ** -0.5)


def _xo_ffn_a_kernel(x_ref, ox_ref, wo_ref, g_ref, wg_ref, wu_ref, x3_ref, h_ref, *, chunk):
    x3 = x_ref[...] + _bdot(ox_ref[...], wo_ref[...])
    x3_ref[...] = x3
    _swiglu_to(_rms(x3, g_ref[...]).astype(BF16), wg_ref, wu_ref, h_ref, chunk)


def _ffn_b_kernel(x_ref, h_ref, wd_ref, o_ref):
    o_ref[...] = x_ref[...] + 0.5 * jnp.dot(h_ref[...], wd_ref[...], preferred_element_type=F32)


def _norm_kernel(x_ref, g_ref, o_ref):
    o_ref[...] = _rms(x_ref[...], g_ref[...])


def _norm_proj_kernel(x_ref, g_ref, w_ref, o_ref):
    o_ref[...] = _bdot(_rms(x_ref[...], g_ref[...]), w_ref[...])


def _token_call(body, tiled_in, full_in, out_widths, out_dtypes, tm):
    m = tiled_in[0].shape[0]
    assert m % tm == 0
    in_specs = [pl.BlockSpec((tm, a.shape[1]), lambda i: (i, 0)) for a in tiled_in]
    in_specs += [_full(a.shape) for a in full_in]
    out_specs = [pl.BlockSpec((tm, w), lambda i: (i, 0)) for w in out_widths]
    out_shape = [jax.ShapeDtypeStruct((m, w), dt) for w, dt in zip(out_widths, out_dtypes)]
    return pl.pallas_call(body, grid=(m // tm,), in_specs=in_specs, out_specs=out_specs, out_shape=out_shape,
                          compiler_params=_cparams(("parallel",)))(*tiled_in, *full_in)


def _ff_chunk(d_ff):
    return 256 if d_ff % 256 == 0 else LANES


def _xattn_kernel(q_ref, kv_ref, o_ref, *, groups, lq):
    for g in range(groups):
        rows = slice(g * lq, (g + 1) * lq)
        for h in range(N_HEADS):
            cols = slice(h * HEAD_DIM, (h + 1) * HEAD_DIM)
            k = kv_ref[g, :, cols]
            v = kv_ref[g, :, GROUP_W + h * HEAD_DIM:GROUP_W + (h + 1) * HEAD_DIM]
            s = _bdot_t(q_ref[rows, cols], k)
            e = jnp.exp(s - jnp.max(s, axis=-1, keepdims=True))
            p = e / jnp.sum(e, axis=-1, keepdims=True)
            o_ref[rows, cols] = _bdot(p, v)


def _xattn_stacked_kernel(q_ref, kv_ref, o_ref, *, groups, lq):
    head = lax.broadcasted_iota(jnp.int32, (lq, GROUP_W), 1) // HEAD_DIM
    for g in range(groups):
        rows = slice(g * lq, (g + 1) * lq)
        q = q_ref[rows, :]
        qs = jnp.concatenate([jnp.where(head == h, q, 0.0) for h in range(N_HEADS)], axis=0)
        s = _bdot_t(qs, kv_ref[g, :, 0:GROUP_W])
        e = jnp.exp(s - jnp.max(s, axis=-1, keepdims=True))
        p = e / jnp.sum(e, axis=-1, keepdims=True)
        o = _bdot(p, kv_ref[g, :, GROUP_W:2 * GROUP_W])
        out = jnp.zeros((lq, GROUP_W), F32)
        for h in range(N_HEADS):
            out = jnp.where(head == h, o[h * lq:(h + 1) * lq], out)
        o_ref[rows, :] = out


def _cross_attend(q, row0, l, mem_kv, groups, lq_tile):
    b, n_mem, _ = mem_kv.shape
    if groups > 1:
        assert lq_tile == l and b % groups == 0 and row0 % (groups * l) == 0
        blk0 = row0 // (groups * l)
        grid = (b // groups,)
        q_spec = pl.BlockSpec((groups * l, GROUP_W), lambda i: (blk0 + i, 0))
        o_spec = pl.BlockSpec((groups * l, GROUP_W), lambda i: (i, 0))
        kv_spec = pl.BlockSpec((groups, n_mem, 2 * GROUP_W), lambda i: (i, 0, 0))
        body = _xattn_stacked_kernel
    else:
        assert l % lq_tile == 0 and row0 % lq_tile == 0
        nq = l // lq_tile
        blk0 = row0 // lq_tile
        grid = (b, nq)
        q_spec = pl.BlockSpec((lq_tile, GROUP_W), lambda bi, i: (blk0 + bi * nq + i, 0))
        o_spec = pl.BlockSpec((lq_tile, GROUP_W), lambda bi, i: (bi * nq + i, 0))
        kv_spec = pl.BlockSpec((1, n_mem, 2 * GROUP_W), lambda bi, i: (bi, 0, 0))
        body = _xattn_kernel
    return pl.pallas_call(
        functools.partial(body, groups=groups, lq=lq_tile), grid=grid, in_specs=[q_spec, kv_spec],
        out_specs=o_spec, out_shape=jax.ShapeDtypeStruct((b * l, GROUP_W), F32),
        compiler_params=_cparams(("parallel",) * len(grid)))(q, mem_kv)


def _rw_prep_kernel(p_ref, prev_ref, mu_ref, vec_ref, w2_ref, a2_ref, g2_ref,
                    rp_ref, nkk_ref, w_ref, kka_ref, k_ref, v_ref, ovk_ref, g_ref, bonus_ref, carry_ref, *, seqs):
    p = p_ref[...]
    tl = p.shape[0]
    row = lax.broadcasted_iota(jnp.int32, p.shape, 0)
    if seqs == 1:
        @pl.when(pl.program_id(1) == 0)
        def _():
            carry_ref[...] = prev_ref[0]

        prev = jnp.where(row == 0, carry_ref[...], pltpu.roll(p, 1, axis=0))
        carry_ref[...] = p[tl - 1:tl, :]
    else:
        seq = tl // seqs
        first = jnp.broadcast_to(prev_ref[...], (seqs, seq, p.shape[1])).reshape(tl, p.shape[1])
        prev = jnp.where(row % seq == 0, first, pltpu.roll(p, 1, axis=0))
    xs = p + (prev - p) * mu_ref[...]
    r, k, v = xs[:, 0:GROUP_W], xs[:, GROUP_W:2 * GROUP_W], xs[:, 2 * GROUP_W:3 * GROUP_W]
    o = 3 * GROUP_W
    xw, xa, xg = xs[:, o:o + W_LORA], xs[:, o + W_LORA:o + W_LORA + A_LORA], xs[:, o + W_LORA + A_LORA:]
    w0, a0, k_k, k_a, r_k = (vec_ref[i:i + 1, :] for i in range(5))
    z = -(w0 + _bdot(jnp.tanh(xw), w2_ref[...]))
    softplus = jnp.maximum(z, 0.0) + jnp.log1p(jnp.exp(-jnp.abs(z)))
    decay = jnp.exp(-jnp.exp(-softplus - 0.5))
    a = jax.nn.sigmoid(a0 + _bdot(xa, a2_ref[...]))
    g_ref[...] = _bdot(jax.nn.sigmoid(xg), g2_ref[...])
    ones = _seg_ones(GROUP_W, HEAD_DIM)
    seg = lambda t: jnp.dot(t, ones, precision=HIGHEST, preferred_element_type=F32)
    kk = k * k_k
    kk = kk / jnp.maximum(jnp.sqrt(seg(kk * kk)), 1e-12)
    k_eff = k * (1.0 + (a - 1.0) * k_a)
    nkk = -kk
    kka = kk * a
    rp_ref[...] = decay * r + nkk * seg(kka * r)
    ovk_ref[...] = v * seg(k_eff * r)
    bonus_ref[...] = seg(r * k_eff * r_k) * v
    nkk_ref[...] = nkk
    w_ref[...] = decay
    kka_ref[...] = kka
    k_ref[...] = k_eff
    v_ref[...] = v


def _rw_scan_kernel(rp_ref, nkk_ref, w_ref, kka_ref, k_ref, v_ref, ovk_ref, g_ref, bonus_ref, s0_ref, vec_ref,
                    out_ref, sfin_ref, s_ref, o_ref, *, nb, tc):
    j = pl.program_id(1)

    @pl.when(j == 0)
    def _():
        s_ref[...] = s0_ref[...]

    pairs = [(b, p) for b in range(nb) for p in range(N_HEADS // 2)]
    left1 = lax.broadcasted_iota(jnp.int32, (1, 1, LANES), 2) < HEAD_DIM
    left = lax.broadcasted_iota(jnp.int32, (1, HEAD_DIM, LANES), 2) < HEAD_DIM
    eye2 = (lax.broadcasted_iota(jnp.int32, (1, HEAD_DIM, LANES), 1)
            == lax.broadcasted_iota(jnp.int32, (1, HEAD_DIM, LANES), 2) % HEAD_DIM).astype(F32)
    rsum = lambda t: jnp.sum(t, axis=2, keepdims=True)
    head_ones = _seg_ones(LANES, HEAD_DIM).astype(BF16)

    def halves(rw):
        return jnp.where(left1, rw, 0.0), jnp.where(left1, 0.0, rw)

    def head_sums(x):
        y = jnp.dot(x.reshape(len(pairs) * HEAD_DIM, LANES).astype(BF16), head_ones, preferred_element_type=F32)
        return y.reshape(len(pairs), HEAD_DIM, LANES)

    def bf16_pieces(x):
        hi = x.astype(BF16).astype(F32)
        mid = (x - hi).astype(BF16).astype(F32)
        return hi, mid, x - hi - mid

    def step(t8, carry):
        base = pl.multiple_of(t8 * SUBLANES, SUBLANES)
        ld = lambda ref: jnp.stack([ref[b, pl.ds(base, SUBLANES), p * LANES:(p + 1) * LANES] for b, p in pairs])
        nkk, rp, v, w, kka, k = (ld(r) for r in (nkk_ref, rp_ref, v_ref, w_ref, kka_ref, k_ref))
        s = s_ref[...].reshape(len(pairs), HEAD_DIM, LANES)
        v3 = bf16_pieces(v)
        vcs = [sum(head_sums(eye2 * piece[:, i:i + 1, :]) for piece in v3) for i in range(SUBLANES)]
        o_rows = []
        for i in range(SUBLANES):
            row = lambda x: x[:, i:i + 1, :]
            nkk_l, nkk_r = halves(row(nkk))
            sa = jnp.where(left, rsum(s * nkk_l), rsum(s * nkk_r))
            oc = head_sums(s * row(rp))
            s = s * row(w) + sa * row(kka) + vcs[i] * row(k)
            o_rows.append(jnp.sum(oc * eye2, axis=1, keepdims=True))
        s_ref[...] = s.reshape(s_ref.shape)
        o_new = jnp.concatenate(o_rows, axis=1) + ld(ovk_ref)
        for n, (b, p) in enumerate(pairs):
            o_ref[b, pl.ds(base, SUBLANES), p * LANES:(p + 1) * LANES] = o_new[n]
        return carry

    lax.fori_loop(0, tc // SUBLANES, step, 0)

    @pl.when(j == pl.num_programs(1) - 1)
    def _():
        sfin_ref[...] = s_ref[...]

    mean = _seg_ones(GROUP_W, HEAD_DIM) * (1.0 / HEAD_DIM)
    seg_mean = lambda t: jnp.dot(t, mean, precision=HIGHEST, preferred_element_type=F32)
    ln_w, ln_b = vec_ref[5:6, :], vec_ref[6:7, :]
    for b in range(nb):
        o = o_ref[b]
        d = o - seg_mean(o)
        on = d * lax.rsqrt(seg_mean(d * d) + RW_GN_EPS) * ln_w + ln_b
        out_ref[b] = (on + bonus_ref[b]) * g_ref[b]


def _rwkv_mix(p, row0, l, p_prev, s0, lw, nb, tc, tl):
    b = p_prev.shape[0]
    assert l % tc == 0 and b % nb == 0 and tc % SUBLANES == 0 and row0 % tl == 0
    seqs_per_tile = max(tl // l, 1)
    assert (l % tl == 0) if seqs_per_tile == 1 else (tl % l == 0 and b % seqs_per_tile == 0)
    nt = max(l // tl, 1)
    blk0 = row0 // tl
    tile = pl.BlockSpec((tl, GROUP_W), lambda bi, j: (bi * nt + j, 0))
    outs = pl.pallas_call(
        functools.partial(_rw_prep_kernel, seqs=seqs_per_tile), grid=(b // seqs_per_tile, nt),
        in_specs=[pl.BlockSpec((tl, RW_COLS), lambda bi, j: (blk0 + bi * nt + j, 0)),
                  pl.BlockSpec((seqs_per_tile, 1, RW_COLS), lambda bi, j: (bi, 0, 0)),
                  _full((1, RW_COLS)), _full((7, GROUP_W)), _full((W_LORA, GROUP_W)), _full((A_LORA, GROUP_W)),
                  _full((G_LORA, GROUP_W))],
        out_specs=[tile] * 9, out_shape=[jax.ShapeDtypeStruct((b * l, GROUP_W), F32)] * 9,
        scratch_shapes=[pltpu.VMEM((1, RW_COLS), F32)],
        compiler_params=_cparams(("arbitrary", "arbitrary")))(
            p, p_prev[:, None, :], lw['rw_mu'], lw['rw_vec'], lw['rw_w2'], lw['rw_a2'], lw['rw_g2'])
    seqs = [a.reshape(b, l, GROUP_W) for a in outs]
    s0p = s0.reshape(b, 2, 2, HEAD_DIM, HEAD_DIM).transpose(0, 1, 3, 2, 4).reshape(b, 2, HEAD_DIM, LANES)
    seq_spec = pl.BlockSpec((nb, tc, GROUP_W), lambda bi, j: (bi, j, 0))
    st_spec = pl.BlockSpec((nb, 2, HEAD_DIM, LANES), lambda bi, j: (bi, 0, 0, 0))
    out, sfin = pl.pallas_call(
        functools.partial(_rw_scan_kernel, nb=nb, tc=tc), grid=(b // nb, l // tc),
        in_specs=[seq_spec] * 9 + [st_spec, _full((7, GROUP_W))],
        out_specs=[seq_spec, st_spec],
        out_shape=[jax.ShapeDtypeStruct((b, l, GROUP_W), F32), jax.ShapeDtypeStruct(s0p.shape, F32)],
        scratch_shapes=[pltpu.VMEM((nb, 2, HEAD_DIM, LANES), F32), pltpu.VMEM((nb, tc, GROUP_W), F32)],
        compiler_params=_cparams(("arbitrary", "arbitrary")))(*seqs, s0p, lw['rw_vec'])
    sfin = sfin.reshape(b, 2, HEAD_DIM, 2, HEAD_DIM).transpose(0, 1, 3, 2, 4).reshape(b, N_HEADS, HEAD_DIM, HEAD_DIM)
    return out.reshape(b * l, GROUP_W), sfin


HALO = 32


def _conv_kernel(u_ref, buf_ref, dw_ref, vec_ref, pw_ref, y_ref, st_ref, hp_ref, *, tl):
    j = pl.program_id(1)

    @pl.when(j == 0)
    def _():
        hp_ref[0:HALO - (CONV_W - 1), :] = jnp.zeros((HALO - (CONV_W - 1), GROUP_W), F32)
        hp_ref[HALO - (CONV_W - 1):HALO, :] = buf_ref[...]

    @pl.when(j > 0)
    def _():
        hp_ref[0:HALO, :] = hp_ref[tl:tl + HALO, :]

    u = u_ref[...]
    hp_ref[HALO:HALO + tl, :] = u[:, 0:GROUP_W] * jax.nn.sigmoid(u[:, GROUP_W:])
    acc = jnp.zeros((tl, GROUP_W), F32)
    for i in range(CONV_W):
        o = HALO - (CONV_W - 1) + i
        acc = acc + hp_ref[o:o + tl, :] * dw_ref[i:i + 1, :]
    y = acc + vec_ref[0:1, :]
    d = y - jnp.mean(y, axis=-1, keepdims=True)
    y = d * lax.rsqrt(jnp.mean(d * d, axis=-1, keepdims=True) + CONV_LN_EPS) * vec_ref[1:2, :] + vec_ref[2:3, :]
    y_ref[...] = _bdot(y * jax.nn.sigmoid(y), pw_ref[...])

    @pl.when(j == pl.num_programs(1) - 1)
    def _():
        st_ref[...] = hp_ref[tl + HALO - (CONV_W - 1):tl + HALO, :]


def _conv_mix(u, row0, l, buf, lw, tl):
    b = buf.shape[0]
    assert l % tl == 0 and row0 % tl == 0
    nt = l // tl
    blk0 = row0 // tl
    return pl.pallas_call(
        functools.partial(_conv_kernel, tl=tl), grid=(b, nt),
        in_specs=[pl.BlockSpec((tl, 2 * GROUP_W), lambda bi, j: (blk0 + bi * nt + j, 0)),
                  pl.BlockSpec((None, CONV_W - 1, GROUP_W), lambda bi, j: (bi, 0, 0)),
                  _full((CONV_W, GROUP_W)), _full((3, GROUP_W)), _full((GROUP_W, GROUP_W))],
        out_specs=[pl.BlockSpec((tl, GROUP_W), lambda bi, j: (bi * nt + j, 0)),
                   pl.BlockSpec((None, CONV_W - 1, GROUP_W), lambda bi, j: (bi, 0, 0))],
        out_shape=[jax.ShapeDtypeStruct((b * l, GROUP_W), F32), jax.ShapeDtypeStruct(buf.shape, F32)],
        scratch_shapes=[pltpu.VMEM((tl + HALO, GROUP_W), F32)],
        compiler_params=_cparams(("arbitrary", "arbitrary")))(u, buf, lw['conv_dw'], lw['conv_vec'], lw['conv_pw'])


def _mla_prep_kernel(p_ref, cos_ref, sin_ref, qn_ref, kvn_ref, wqn_ref, wqr_ref, wqp_ref, wkb_ref,
                     rows_ref, keys_ref, q_ref):
    p = p_ref[...]
    cos, sin = cos_ref[...], sin_ref[...]
    c_q = _rms(p[:, 0:LANES], qn_ref[...]).astype(BF16)
    c_kv = _rms(p[:, LANES:2 * LANES], kvn_ref[...])
    k_rope = p[:, 2 * LANES:3 * LANES] * cos + p[:, 3 * LANES:4 * LANES] * sin
    rows_ref[:, 0:KV_LORA] = c_kv
    rows_ref[:, KV_LORA:C_MLA] = k_rope[:, 0:QK_ROPE]
    keys_ref[...] = jnp.concatenate([c_kv, k_rope], axis=-1).astype(BF16)
    for h in range(N_HEADS):
        q_nope = jnp.dot(c_q, wqn_ref[h], preferred_element_type=F32)
        q_lat = _bdot(q_nope, wkb_ref[h])
        q_rope = (jnp.dot(c_q, wqr_ref[h], preferred_element_type=F32) * cos
                  + jnp.dot(c_q, wqp_ref[h], preferred_element_type=F32) * sin)
        q_ref[h] = jnp.concatenate([q_lat, q_rope], axis=-1) * MLA_SCALE


def _mla_prep(p_mla, cos, sin, lw, tm):
    m = p_mla.shape[0]
    assert m % tm == 0
    row = lambda w: pl.BlockSpec((tm, w), lambda i: (i, 0))
    return pl.pallas_call(
        _mla_prep_kernel, grid=(m // tm,),
        in_specs=[row(MLA_W), row(LANES), row(LANES), _full((1, LANES)), _full((1, LANES)),
                  _full((N_HEADS, LANES, QK_NOPE)), _full((N_HEADS, LANES, LANES)), _full((N_HEADS, LANES, LANES)),
                  _full((N_HEADS, QK_NOPE, KV_LORA))],
        out_specs=[row(C_MLA), row(2 * LANES), pl.BlockSpec((N_HEADS, tm, 2 * LANES), lambda i: (0, i, 0))],
        out_shape=[jax.ShapeDtypeStruct((m, C_MLA), F32), jax.ShapeDtypeStruct((m, 2 * LANES), BF16),
                   jax.ShapeDtypeStruct((N_HEADS, m, 2 * LANES), F32)],
        compiler_params=_cparams(("parallel",)))(
            p_mla, cos, sin, lw['mla_q_norm'], lw['mla_kv_norm'], lw['mla_wq_nope'], lw['mla_wq_rope'],
            lw['mla_wq_rope_rot'], lw['mla_wkb'])


def _mla_out(o_lat, wvb_ref, rows_per_head):
    out = None
    for h in range(N_HEADS):
        t = _bdot(o_lat[h * rows_per_head:(h + 1) * rows_per_head], wvb_ref[h])
        out = t if out is None else out + t
    return out


def _mla_prompt_kernel(q_ref, keys_ref, wvb_ref, o_ref, *, tq):
    i = pl.program_id(1)
    q = q_ref[...].reshape(N_HEADS * tq, 2 * LANES).astype(BF16)
    t = i * tq + lax.broadcasted_iota(jnp.int32, (N_HEADS * tq, tq), 0) % tq
    col = lax.broadcasted_iota(jnp.int32, (N_HEADS * tq, tq), 1)

    def body(kt, carry, diagonal=False):
        m, l, acc = carry
        k = keys_ref[pl.ds(pl.multiple_of(kt * tq, tq), tq), :]
        s = _bdot_t(q, k)
        if diagonal:
            mask = kt * tq + col <= t
            s = jnp.where(mask, s, NEG)
        m_new = jnp.maximum(m, jnp.max(s, axis=-1, keepdims=True))
        e = jnp.exp(s - m_new)
        if diagonal:
            e = jnp.where(mask, e, 0.0)
        alpha = jnp.exp(m - m_new)
        return m_new, alpha * l + jnp.sum(e, axis=-1, keepdims=True), alpha * acc + _bdot(e, k[:, 0:KV_LORA])

    init = (jnp.full((N_HEADS * tq, 1), NEG, F32), jnp.zeros((N_HEADS * tq, 1), F32),
            jnp.zeros((N_HEADS * tq, KV_LORA), F32))
    m, l, acc = body(i, lax.fori_loop(0, i, body, init), diagonal=True)
    o_ref[...] = _mla_out(acc / l, wvb_ref, tq)


def _mla_prompt(q, keys, wvb, b, l, tq):
    m = b * l
    assert l % tq == 0
    nq = l // tq
    return pl.pallas_call(
        functools.partial(_mla_prompt_kernel, tq=tq), grid=(b, nq),
        in_specs=[pl.BlockSpec((N_HEADS, tq, 2 * LANES), lambda bi, i: (0, bi * nq + i, 0)),
                  pl.BlockSpec((l, 2 * LANES), lambda bi, i: (bi, 0)), _full(wvb.shape)],
        out_specs=pl.BlockSpec((tq, GROUP_W), lambda bi, i: (bi * nq + i, 0)),
        out_shape=jax.ShapeDtypeStruct((m, GROUP_W), F32),
        compiler_params=_cparams(("parallel", "parallel")))(q, keys, wvb)


def _page_copies(table_ref, windows, sem_ref, b, slot, n_pages):
    return [pltpu.make_async_copy(src, dst, sem_ref.at[slot])
            for j in range(n_pages) for src, dst in windows(table_ref[b, j], slot, j)]


def _paged_fetch(table_ref, windows, sem_ref, n_pages):
    b = pl.program_id(0)
    slot = b % 2

    @pl.when(b == 0)
    def _():
        for c in _page_copies(table_ref, windows, sem_ref, b, slot, n_pages):
            c.start()

    @pl.when(b + 1 < pl.num_programs(0))
    def _():
        for c in _page_copies(table_ref, windows, sem_ref, b + 1, 1 - slot, n_pages):
            c.start()

    for c in _page_copies(table_ref, windows, sem_ref, b, slot, n_pages):
        c.wait()
    return slot


def _mla_sample_kernel(table_ref, q_ref, new_ref, pool_ref, wvb_ref, o_ref, buf_ref, sem_ref, *, layer, n_pages, lq):
    page = buf_ref.shape[2] // n_pages
    windows = lambda pid, slot, j: [(pool_ref.at[layer, pid], buf_ref.at[slot, :, pl.ds(j * page, page)])]
    slot = _paged_fetch(table_ref, windows, sem_ref, n_pages)
    past_t = buf_ref[slot].astype(BF16)
    new = new_ref[...].astype(BF16)
    rows = N_HEADS * lq
    q = q_ref[...].reshape(rows, 2 * LANES)[:, 0:C_MLA].astype(BF16)
    s_past = jnp.dot(q, past_t, preferred_element_type=F32)
    qi = lax.broadcasted_iota(jnp.int32, (rows, lq), 0) % lq
    mask = lax.broadcasted_iota(jnp.int32, (rows, lq), 1) <= qi
    s_new = jnp.where(mask, _bdot_t(q, new), NEG)
    m = jnp.maximum(jnp.max(s_past, axis=-1, keepdims=True), jnp.max(s_new, axis=-1, keepdims=True))
    e_past = jnp.exp(s_past - m)
    e_new = jnp.where(mask, jnp.exp(s_new - m), 0.0)
    den = jnp.sum(e_past, axis=-1, keepdims=True) + jnp.sum(e_new, axis=-1, keepdims=True)
    o_lat = (_bdot_t(e_past, past_t[0:KV_LORA, :]) + _bdot(e_new, new[:, 0:KV_LORA])) / den
    o_ref[...] = _mla_out(o_lat, wvb_ref, lq)


def _mla_sample(q, new_rows, pool_t, page_table, wvb, layer, row0):
    b, n_pages = page_table.shape
    page = pool_t.shape[3]
    lq = (q.shape[1] - row0) // b
    assert row0 % lq == 0 and page % LANES == 0
    blk0 = row0 // lq
    grid_spec = pltpu.PrefetchScalarGridSpec(
        num_scalar_prefetch=1, grid=(b,),
        in_specs=[pl.BlockSpec((N_HEADS, lq, 2 * LANES), lambda i, tbl: (0, blk0 + i, 0)),
                  pl.BlockSpec((lq, C_MLA), lambda i, tbl: (blk0 + i, 0)),
                  pl.BlockSpec(memory_space=pl.ANY),
                  pl.BlockSpec(wvb.shape, lambda i, tbl: (0, 0, 0))],
        out_specs=pl.BlockSpec((lq, GROUP_W), lambda i, tbl: (i, 0)),
        scratch_shapes=[pltpu.VMEM((2, C_MLA, n_pages * page), F32), pltpu.SemaphoreType.DMA((2,))])
    return pl.pallas_call(
        functools.partial(_mla_sample_kernel, layer=layer, n_pages=n_pages, lq=lq), grid_spec=grid_spec,
        out_shape=jax.ShapeDtypeStruct((b * lq, GROUP_W), F32),
        compiler_params=_cparams(("arbitrary",)))(page_table, q, new_rows, pool_t, wvb)


def _gelu_tanh(x):
    return 0.5 * x * (1.0 + jnp.tanh(0.7978845608028654 * (x + 0.044715 * x * x * x)))


def _compress(a, wexp_ref, pos_ref, b1_ref, w2k_ref, w2v_ref, b2_ref):
    n_ch = a.shape[0]
    c = jnp.dot(pos_ref[...], wexp_ref[...], preferred_element_type=F32)
    up = lambda t: pltpu.roll(t, n_ch - 1, axis=0)
    hk = a[:, 0:LANES] + up(a[:, LANES:2 * LANES]) + c[0:1, 0:LANES] + c[1:2, LANES:2 * LANES] + b1_ref[:, 0:LANES]
    hv = (a[:, 2 * LANES:3 * LANES] + up(a[:, 3 * LANES:4 * LANES]) + c[0:1, 2 * LANES:3 * LANES]
          + c[1:2, 3 * LANES:4 * LANES] + b1_ref[:, LANES:2 * LANES])
    out = _bdot(_gelu_tanh(hk), w2k_ref[...]) + _bdot(_gelu_tanh(hv), w2v_ref[...]) + b2_ref[...]
    valid = lax.broadcasted_iota(jnp.int32, out.shape, 0) < n_ch - 1
    return jnp.where(valid, out, 0.0)


def _compress_kernel(ch_ref, wexp_ref, pos_ref, b1_ref, w2k_ref, w2v_ref, b2_ref, o_ref):
    a = jnp.dot(ch_ref[...].astype(BF16), wexp_ref[...], preferred_element_type=F32)
    o_ref[...] = _compress(a, wexp_ref, pos_ref, b1_ref, w2k_ref, w2v_ref, b2_ref)


def _stack_heads(q, scale):
    low = lax.broadcasted_iota(jnp.int32, (q.shape[0], LANES), 1) < HEAD_DIM
    parts = []
    for h in range(N_HEADS):
        slab = q[:, (h // 2) * LANES:(h // 2 + 1) * LANES]
        if h % 2:
            slab = pltpu.roll(slab, HEAD_DIM, axis=1)
        parts.append(jnp.where(low, slab * scale, 0.0))
    return jnp.concatenate(parts, axis=0).astype(BF16)


def _softmax_parts(parts):
    m = None
    for s, mask in parts:
        mx = jnp.max(jnp.where(mask, s, NEG), axis=-1, keepdims=True)
        m = mx if m is None else jnp.maximum(m, mx)
    es = [jnp.where(mask, jnp.exp(jnp.where(mask, s, NEG) - m), 0.0) for s, mask in parts]
    den = sum(jnp.sum(e, axis=-1, keepdims=True) for e in es)
    return es, jnp.where(den > 0.0, den, 1.0)


def _select_blocks(p_c, t, n_cb, n_sb, nq, fillers=()):
    ncbp = p_c.shape[1]
    nsbp = -(-n_sb // LANES) * LANES
    psum = p_c[0:nq] + p_c[nq:2 * nq] + p_c[2 * nq:3 * nq] + p_c[3 * nq:4 * nq]
    ci = lax.broadcasted_iota(jnp.int32, (ncbp, nsbp), 0)
    sj = lax.broadcasted_iota(jnp.int32, (ncbp, nsbp), 1)
    overlap = ((CMP_STRIDE * ci < SLC_BLOCK * (sj + 1)) & (CMP_STRIDE * ci + CMP_LEN > SLC_BLOCK * sj)
               & (ci < n_cb) & (sj < n_sb)).astype(BF16)
    hi = psum.astype(BF16)
    lo = (psum - hi.astype(F32)).astype(BF16)
    imp = jnp.dot(hi, overlap, preferred_element_type=F32) + jnp.dot(lo, overlap, preferred_element_type=F32)
    j = lax.broadcasted_iota(jnp.int32, (nq, nsbp), 1)
    cur = t // SLC_BLOCK
    forced = (j == 0) | (j == cur) | (j == cur - 1)
    score = jnp.where(forced, FORCE_SCORE, jnp.where(j <= cur, imp, -1.0))
    score = jnp.where(j < n_sb, score, -3e38)
    sel = jnp.zeros((nq, nsbp), F32)
    fillers = list(fillers)
    filled = []
    rounds = min(N_SELECT, n_sb)
    for r in range(rounds):
        mx = jnp.max(score, axis=-1, keepdims=True)
        first = jnp.min(jnp.where(score == mx, j, nsbp), axis=-1, keepdims=True)
        hit = j == first
        sel = jnp.where(hit, 1.0, sel)
        score = jnp.where(hit, -3e38, score)
        share = -(-len(fillers) // (rounds - r))
        filled += [f() for f in fillers[:share]]
        fillers = fillers[share:]
    return sel.astype(BF16), filled


def _tile4(x):
    return jnp.concatenate([x] * N_HEADS, axis=0)


def _gate_combine(gates, o_c, o_s, o_w, o_ref, nq):
    for h in range(N_HEADS):
        rows = slice(h * nq, (h + 1) * nq)
        g = lambda c: gates[:, 3 * h + c:3 * h + c + 1]
        o = g(0) * o_c[rows] + g(1) * o_s[rows] + g(2) * o_w[rows]
        o_ref[:, h * HEAD_DIM:(h + 1) * HEAD_DIM] = o[:, HEAD_DIM:LANES]


def _nsa_prompt_kernel(q_ref, g_ref, cmp_ref, slc_ref, win_ref, o_ref, slc_s, win_s, *, tq, l, tks):
    i = pl.program_id(1)

    @pl.when(i == 0)
    def _():
        slc_s[...] = slc_ref[...].astype(BF16)
        win_s[...] = win_ref[...].astype(BF16)

    rows = N_HEADS * tq
    n_cb, n_sb = l // CMP_STRIDE - 1, l // SLC_BLOCK
    q = _stack_heads(q_ref[...], HEAD_DIM ** -0.5)
    tq1 = i * tq + lax.broadcasted_iota(jnp.int32, (tq, 1), 0)
    t = _tile4(tq1)
    kcvc = cmp_ref[...].astype(BF16)
    n = lax.broadcasted_iota(jnp.int32, (rows, kcvc.shape[0]), 1)
    (e_c,), den = _softmax_parts([(_bdot_t(q, kcvc), (CMP_STRIDE * n + CMP_LEN <= t + 1) & (n < n_cb))])
    p_c = e_c / den
    span = WINDOW + tq
    w0 = pl.multiple_of(jnp.maximum(i * tq - WINDOW, 0), tq)
    wpos = w0 + lax.broadcasted_iota(jnp.int32, (tq, span), 1)
    wmask = (wpos <= tq1) & (wpos > tq1 - WINDOW)

    def window_head(h):
        kvw = win_s[pl.ds(w0, span), :]
        (e_w,), den_w = _softmax_parts([(_bdot_t(q[h * tq:(h + 1) * tq], kvw), wmask)])
        return _bdot(e_w, kvw) / den_w

    fillers = [functools.partial(window_head, h) for h in range(N_HEADS)] + [lambda: _bdot(p_c, kcvc)]
    sel, (*o_w, o_c) = _select_blocks(p_c, tq1, n_cb, n_sb, tq, fillers)
    o_w = jnp.concatenate(o_w, axis=0)
    blk = lax.broadcasted_iota(jnp.int32, (sel.shape[1], tks), 0)
    kcol = lax.broadcasted_iota(jnp.int32, (sel.shape[1], tks), 1)
    col = lax.broadcasted_iota(jnp.int32, (rows, tks), 1)

    def body(kt, carry):
        m, lsum, acc = carry
        k0 = pl.multiple_of(kt * tks, tks)
        kv = slc_s[pl.ds(k0, tks), :]
        expand = (blk == (k0 + kcol) // SLC_BLOCK).astype(BF16)
        chosen = _tile4(jnp.dot(sel, expand, preferred_element_type=F32)) > 0.5
        mask = chosen & (k0 + col <= t)
        s = jnp.where(mask, _bdot_t(q, kv), NEG)
        m_new = jnp.maximum(m, jnp.max(s, axis=-1, keepdims=True))
        e = jnp.where(mask, jnp.exp(s - m_new), 0.0)
        alpha = jnp.exp(m - m_new)
        return m_new, alpha * lsum + jnp.sum(e, axis=-1, keepdims=True), alpha * acc + _bdot(e, kv)

    init = (jnp.full((rows, 1), NEG, F32), jnp.zeros((rows, 1), F32), jnp.zeros((rows, LANES), F32))
    _, lsum, acc = lax.fori_loop(0, ((i + 1) * tq + tks - 1) // tks, body, init)
    o_s = acc / jnp.where(lsum > 0.0, lsum, 1.0)
    _gate_combine(jax.nn.sigmoid(g_ref[...]), o_c, o_s, o_w, o_ref, tq)


def _nsa_prompt(p_nsa, lw, b, l, tq):
    m = b * l
    n_ch = l // CMP_STRIDE
    tks = 4 * SLC_BLOCK
    assert l % tq == 0 and l >= WINDOW + tq and l % tks == 0 and tq % SLC_BLOCK == 0 and WINDOW % tq == 0
    ch = p_nsa[:m, GROUP_W:GROUP_W + LANES].reshape(m // CMP_STRIDE, CMP_STRIDE * LANES)
    kcvc = pl.pallas_call(
        _compress_kernel, grid=(b,),
        in_specs=[pl.BlockSpec((n_ch, CMP_STRIDE * LANES), lambda bi: (bi, 0)), _full(lw['cmp_wexp_p'].shape),
                  _full(lw['cmp_pos_p'].shape), _full((1, 2 * LANES)), _full((LANES, LANES)), _full((LANES, LANES)),
                  _full((1, LANES))],
        out_specs=pl.BlockSpec((n_ch, LANES), lambda bi: (bi, 0)),
        out_shape=jax.ShapeDtypeStruct((b * n_ch, LANES), F32),
        compiler_params=_cparams(("parallel",)))(
            ch, lw['cmp_wexp_p'], lw['cmp_pos_p'], lw['cmp_b1'], lw['cmp_w2k'], lw['cmp_w2v'], lw['cmp_b2'])
    nq = l // tq
    slab = lambda c: pl.BlockSpec((l, LANES), lambda bi, i: (bi, c))
    return pl.pallas_call(
        functools.partial(_nsa_prompt_kernel, tq=tq, l=l, tks=tks), grid=(b, nq),
        in_specs=[pl.BlockSpec((tq, GROUP_W), lambda bi, i: (bi * nq + i, 0)),
                  pl.BlockSpec((tq, LANES), lambda bi, i: (bi * nq + i, NSA_W // LANES - 1)),
                  pl.BlockSpec((n_ch, LANES), lambda bi, i: (bi, 0)), slab(3), slab(4)],
        out_specs=pl.BlockSpec((tq, GROUP_W), lambda bi, i: (bi * nq + i, 0)),
        out_shape=jax.ShapeDtypeStruct((m, GROUP_W), F32),
        scratch_shapes=[pltpu.VMEM((l, LANES), BF16), pltpu.VMEM((l, LANES), BF16)],
        compiler_params=_cparams(("parallel", "arbitrary")))(p_nsa, p_nsa, kcvc, p_nsa, p_nsa)


def _nsa_sample_kernel(table_ref, q_ref, g_ref, nslc_ref, nwin_ref, win_ref, pool_ref, wexp_ref, pos_ref, b1_ref,
                       w2k_ref, w2v_ref, b2_ref, o_ref, wout_ref, cmp_buf, slc_buf, sem_ref, *, layer, n_pages, lq, past,
                       tks):
    page = past // n_pages

    def windows(pid, slot, j):
        return [(pool_ref.at[layer, pid, :, pl.ds(c * LANES, LANES)], buf.at[slot, pl.ds(j * page, page), :])
                for c, buf in enumerate((cmp_buf, slc_buf))]

    slot = _paged_fetch(table_ref, windows, sem_ref, n_pages)
    n_ch = past // CMP_STRIDE
    rows = N_HEADS * lq
    n_cb = (past + lq) // CMP_STRIDE - 1
    n_sb = -(-(past + lq) // SLC_BLOCK)
    q = _stack_heads(q_ref[...], HEAD_DIM ** -0.5)
    tq1 = past + lax.broadcasted_iota(jnp.int32, (lq, 1), 0)
    t = _tile4(tq1)
    a = jnp.zeros((n_ch, 4 * CMP_HID), F32)
    for r in range(0, CMP_STRIDE, 2):
        x = jnp.concatenate([cmp_buf[slot, pl.ds(r + d, n_ch, stride=CMP_STRIDE), :].astype(BF16) for d in range(2)],
                            axis=1)
        a = a + jnp.dot(x, wexp_ref[r * LANES:(r + 2) * LANES, :], preferred_element_type=F32)
    kcvc = _compress(a, wexp_ref, pos_ref, b1_ref, w2k_ref, w2v_ref, b2_ref).astype(BF16)
    n = lax.broadcasted_iota(jnp.int32, (rows, n_ch), 1)
    (e_c,), den = _softmax_parts([(_bdot_t(q, kcvc), (CMP_STRIDE * n + CMP_LEN <= t + 1) & (n < n_cb))])
    p_c = e_c / den
    qi = lax.broadcasted_iota(jnp.int32, (rows, lq), 0) % lq
    causal_new = lax.broadcasted_iota(jnp.int32, (rows, lq), 1) <= qi

    def window():
        win = win_ref[...]
        nwin = nwin_ref[...]
        wb = win.shape[0]
        wpos = past - wb + lax.broadcasted_iota(jnp.int32, (rows, wb), 1)
        es, den_w = _softmax_parts([(_bdot_t(q, win), (wpos <= t) & (wpos > t - WINDOW) & (wpos >= 0)),
                                    (_bdot_t(q, nwin), causal_new)])
        wout_ref[0:wb - lq, :] = win[lq:wb, :]
        wout_ref[wb - lq:wb, :] = nwin
        return (_bdot(es[0], win) + _bdot(es[1], nwin)) / den_w

    def slc_scores(kt):
        kv = slc_buf[slot, pl.ds(kt * tks, tks), :].astype(BF16)
        return kv, _bdot_t(q, kv)

    fillers = ([window, lambda: _bdot(p_c, kcvc)] + [functools.partial(slc_scores, kt) for kt in range(past // tks)])
    sel, (o_w, o_c, *tiles) = _select_blocks(p_c, tq1, n_cb, n_sb, lq, fillers)
    nsbp = sel.shape[1]
    blk = lax.broadcasted_iota(jnp.int32, (nsbp, tks), 0)
    kcol = lax.broadcasted_iota(jnp.int32, (nsbp, tks), 1)
    parts = []
    for kt, (kv, s) in enumerate(tiles):
        expand = (blk == (kt * tks + kcol) // SLC_BLOCK).astype(BF16)
        parts.append((s, _tile4(jnp.dot(sel, expand, preferred_element_type=F32)) > 0.5))
    nslc = nslc_ref[...].astype(BF16)
    new_blk = past // SLC_BLOCK
    sel_new = _tile4(sel[:, new_blk:new_blk + 1].astype(F32)) > 0.5
    parts.append((_bdot_t(q, nslc), causal_new & sel_new))
    es, den = _softmax_parts(parts)
    acc = _bdot(es[-1], nslc)
    for e, (kv, _) in zip(es[:-1], tiles):
        acc = acc + _bdot(e, kv)
    o_s = acc / den
    _gate_combine(jax.nn.sigmoid(g_ref[...]), o_c, o_s, o_w, o_ref, lq)


def _nsa_sample(p_nsa, row0, pool, page_table, win_cache, lw, layer):
    b, n_pages = page_table.shape
    page, c_in = pool.shape[2:]
    past = n_pages * page
    lq = (p_nsa.shape[0] - row0) // b
    wb = win_cache.shape[1]
    tks = min(past, 1024)
    assert row0 % lq == 0 and page % CMP_STRIDE == 0 and lq < CMP_STRIDE and lq % 8 == 0 and past % tks == 0
    blk0 = row0 // lq
    row = lambda w, c: pl.BlockSpec((lq, w), lambda i, tbl: (blk0 + i, c))
    const = lambda a: pl.BlockSpec(a.shape, lambda i, tbl: (0,) * a.ndim)
    weights = [lw['cmp_wexp_p'], lw['cmp_pos_p'], lw['cmp_b1'], lw['cmp_w2k'], lw['cmp_w2v'], lw['cmp_b2']]
    grid_spec = pltpu.PrefetchScalarGridSpec(
        num_scalar_prefetch=1, grid=(b,),
        in_specs=[row(GROUP_W, 0), row(LANES, NSA_W // LANES - 1), row(LANES, 3), row(LANES, 4),
                  pl.BlockSpec((None, wb, LANES), lambda i, tbl: (i, 0, 0)),
                  pl.BlockSpec(memory_space=pl.ANY)] + [const(w) for w in weights],
        out_specs=[pl.BlockSpec((lq, GROUP_W), lambda i, tbl: (i, 0)),
                   pl.BlockSpec((None, wb, LANES), lambda i, tbl: (i, 0, 0))],
        scratch_shapes=[pltpu.VMEM((2, past, LANES), F32), pltpu.VMEM((2, past, LANES), F32),
                        pltpu.SemaphoreType.DMA((2,))])
    return pl.pallas_call(
        functools.partial(_nsa_sample_kernel, layer=layer, n_pages=n_pages, lq=lq, past=past, tks=tks),
        grid_spec=grid_spec,
        out_shape=[jax.ShapeDtypeStruct((b * lq, GROUP_W), F32), jax.ShapeDtypeStruct(win_cache.shape, F32)],
        compiler_params=_cparams(("arbitrary",)))(page_table, p_nsa, p_nsa, p_nsa, p_nsa, win_cache, pool, *weights)


def _pad_cols(w, width):
    return jnp.pad(w, ((0, 0), (0, width - w.shape[1])))


def _rot_cols(w):
    half = QK_ROPE // 2
    return jnp.concatenate([-w[:, half:], w[:, :half]], axis=1)


def _expand_cmp_w1(w1, channels):
    half = CMP_STRIDE * HEAD_DIM
    out = jnp.zeros((CMP_STRIDE, channels, 4 * CMP_HID), F32)
    for br in range(2):
        for part in range(2):
            blk = w1[br, part * half:(part + 1) * half].reshape(CMP_STRIDE, HEAD_DIM, CMP_HID)
            c0 = (2 * br + part) * CMP_HID
            out = out.at[:, br * HEAD_DIM:(br + 1) * HEAD_DIM, c0:c0 + CMP_HID].set(blk)
    return out.reshape(CMP_STRIDE * channels, 4 * CMP_HID).astype(BF16)


def _expand_cmp_pos(pos, channels):
    out = jnp.zeros((8, CMP_STRIDE, channels), F32)
    for br in range(2):
        for part in range(2):
            out = out.at[part, :, br * HEAD_DIM:(br + 1) * HEAD_DIM].set(
                pos[br, part * CMP_STRIDE:(part + 1) * CMP_STRIDE])
    return out.reshape(8, CMP_STRIDE * channels).astype(BF16)


def _layer_weights(w, l):
    nsa0 = RW_COLS
    mla0 = nsa0 + GROUP_W + 6 * HEAD_DIM + 3 * N_HEADS
    conv0 = mla0 + 2 * LANES + QK_ROPE
    w_in = w['w_in'][l]
    k_rope = w_in[:, mla0 + 2 * LANES:conv0]
    w_in_ext = jnp.concatenate([
        w_in[:, :nsa0], _pad_cols(w_in[:, nsa0:mla0], NSA_W), w_in[:, mla0:mla0 + 2 * LANES],
        _pad_cols(k_rope, LANES), _pad_cols(_rot_cols(k_rope), LANES), w_in[:, conv0:]], axis=1).astype(BF16)
    d_hd = QK_NOPE + QK_ROPE
    wqb = w['mla_w_qb'][l].reshape(-1, N_HEADS, d_hd).transpose(1, 0, 2)
    wq_rope = wqb[:, :, QK_NOPE:]
    pad3 = lambda a: jnp.pad(a, ((0, 0), (0, 0), (0, LANES - a.shape[2]))).astype(BF16)
    wvb = jnp.zeros((N_HEADS, KV_LORA, GROUP_W), F32)
    for h in range(N_HEADS):
        wvb = wvb.at[h, :, h * HEAD_DIM:(h + 1) * HEAD_DIM].set(w['mla_w_vb'][l][:, h, :])
    w2 = w['cmp_w2'][l]
    return dict(
        norms=w['norms'][l][:, None, :],
        ffn_wg=w['ffn_w_gate'][l].astype(BF16), ffn_wu=w['ffn_w_up'][l].astype(BF16),
        ffn_wd=w['ffn_w_down'][l].astype(BF16),
        w_in_ext=w_in_ext, w_out=w['w_out'][l].astype(BF16),
        rw_mu=w['rw_mu'][l][None, :], rw_vec=w['rw_vec'][l], rw_w2=w['rw_w2'][l].astype(BF16),
        rw_a2=w['rw_a2'][l].astype(BF16), rw_g2=w['rw_g2'][l].astype(BF16),
        cmp_wexp_p=_expand_cmp_w1(w['cmp_w1'][l], LANES), cmp_pos_p=_expand_cmp_pos(w['cmp_pos'][l], LANES),
        cmp_b1=w['cmp_b1'][l].reshape(1, 2 * CMP_HID),
        cmp_w2k=_pad_cols(w2[0], LANES).astype(BF16),
        cmp_w2v=jnp.pad(w2[1], ((0, 0), (HEAD_DIM, 0))).astype(BF16),
        cmp_b2=w['cmp_b2'][l].reshape(1, 2 * HEAD_DIM),
        mla_q_norm=w['mla_q_norm'][l][None, :], mla_kv_norm=w['mla_kv_norm'][l][None, :],
        mla_wq_nope=wqb[:, :, :QK_NOPE].astype(BF16), mla_wq_rope=pad3(wq_rope),
        mla_wq_rope_rot=pad3(jnp.concatenate([-wq_rope[:, :, QK_ROPE // 2:], wq_rope[:, :, :QK_ROPE // 2]], axis=2)),
        mla_wkb=w['mla_w_kb'][l].transpose(1, 2, 0).astype(BF16), mla_wvb=wvb.astype(BF16),
        conv_dw=w['conv_dw'][l], conv_vec=w['conv_vec'][l], conv_pw=w['conv_pw'][l].astype(BF16),
        x_wq=w['x_wq'][l].astype(BF16), x_wkv=w['x_wkv'][l].astype(BF16), x_wo=w['x_wo'][l].astype(BF16))


def _rope_tables(pos):
    half = QK_ROPE // 2
    inv = ROPE_THETA ** (-jnp.arange(half, dtype=F32) / half)
    ang = pos.astype(F32)[:, None] * inv
    tile = lambda a: jnp.tile(a, (1, LANES // half))
    return tile(jnp.cos(ang)), tile(jnp.sin(ang))


def _pick(n, *cands):
    for c in cands:
        if n % c == 0:
            return c
    return n


def kernel(x_prompt, x_sample, cache_mla, cache_nsa, cache_nsa_win, cache_mem, state_rwkv, state_rwkv_shift,
           state_conv, page_table, mem_prompt, norms, ffn_w_gate, ffn_w_up, ffn_w_down, w_in, w_out, rw_mu, rw_vec,
           rw_w2, rw_a2, rw_g2, cmp_pos, cmp_w1, cmp_b1, cmp_w2, cmp_b2, mla_q_norm, mla_kv_norm, mla_w_qb, mla_w_kb,
           mla_w_vb, conv_dw, conv_vec, conv_pw, x_wq, x_wkv, x_wo, final_norm):
    w = dict(norms=norms, ffn_w_gate=ffn_w_gate, ffn_w_up=ffn_w_up, ffn_w_down=ffn_w_down, w_in=w_in, w_out=w_out,
             rw_mu=rw_mu, rw_vec=rw_vec, rw_w2=rw_w2, rw_a2=rw_a2, rw_g2=rw_g2, cmp_pos=cmp_pos, cmp_w1=cmp_w1,
             cmp_b1=cmp_b1, cmp_w2=cmp_w2, cmp_b2=cmp_b2, mla_q_norm=mla_q_norm, mla_kv_norm=mla_kv_norm,
             mla_w_qb=mla_w_qb, mla_w_kb=mla_w_kb, mla_w_vb=mla_w_vb, conv_dw=conv_dw, conv_vec=conv_vec,
             conv_pw=conv_pw, x_wq=x_wq, x_wkv=x_wkv, x_wo=x_wo)
    depth = norms.shape[0]
    bp, lp, d = x_prompt.shape
    bs, ls, _ = x_sample.shape
    mp, ms = bp * lp, bs * ls
    m = mp + ms
    n_pages, page = page_table.shape[1], cache_mla.shape[2]
    past = n_pages * page
    d_ff = ffn_w_gate.shape[-1]
    chunk = _ff_chunk(d_ff)
    tm = _pick(m, 512, 256, 128, 64, 8)
    n_mem = mem_prompt.shape[1]

    x = jnp.concatenate([x_prompt.reshape(mp, d), x_sample.reshape(ms, d)], axis=0)
    cos_p, sin_p = _rope_tables(jnp.arange(lp, dtype=jnp.int32))
    cos_s, sin_s = _rope_tables(past + jnp.arange(ls, dtype=jnp.int32))
    cos = jnp.concatenate([jnp.tile(cos_p, (bp, 1)), jnp.tile(cos_s, (bs, 1))], axis=0)
    sin = jnp.concatenate([jnp.tile(sin_p, (bp, 1)), jnp.tile(sin_s, (bs, 1))], axis=0)
    mem_rows = mem_prompt.reshape(bp * n_mem, d)
    cache_mla_t = jnp.swapaxes(cache_mla, 2, 3)
    zeros = lambda *s: jnp.zeros(s, F32)
    sp, ss = [], []
    both = lambda a, c: jnp.concatenate([a, c], axis=0)

    for l in range(depth):
        lw = _layer_weights(w, l)
        n = lw['norms']
        ffn_w = lambda i: (lw['ffn_wg'][i], lw['ffn_wu'][i])
        (h1,) = _token_call(functools.partial(_ffn_a_kernel, chunk=chunk), [x], [n[0], *ffn_w(0)], [d_ff], [BF16], tm)
        x1, p_rw, p_nsa, p_mla, p_conv = _token_call(
            _ffn_b_proj_kernel, [x, h1], [lw['ffn_wd'][0], n[1], lw['w_in_ext']],
            [d, RW_COLS, NSA_W, MLA_W, 2 * GROUP_W], [F32] * 5, tm)
        (mem_kv,) = _token_call(_norm_proj_kernel, [mem_rows], [n[3], lw['x_wkv']], [2 * GROUP_W], [F32],
                                _pick(bp * n_mem, 256, 8))
        mem_kv = mem_kv.reshape(bp, n_mem, 2 * GROUP_W)
        o_rw_p, rw_state_p = _rwkv_mix(p_rw, 0, lp, zeros(bp, RW_COLS), zeros(bp, N_HEADS, HEAD_DIM, HEAD_DIM), lw,
                                       nb=bp, tc=_pick(lp, 256, 8), tl=_pick(lp, 512, 8))
        o_rw_s, rw_state_s = _rwkv_mix(p_rw, mp, ls, state_rwkv_shift[l], state_rwkv[l], lw,
                                       nb=_pick(bs, 4, 1), tc=ls, tl=_pick(ms, 512, 64, ls))
        o_nsa_p = _nsa_prompt(p_nsa, lw, bp, lp, tq=_pick(lp, 256, 128))
        o_nsa_s, win_s = _nsa_sample(p_nsa, mp, cache_nsa, page_table, cache_nsa_win[l], lw, l)
        mla_rows, mla_keys, mla_q = _mla_prep(p_mla, cos, sin, lw, tm)
        o_mla_p = _mla_prompt(mla_q, mla_keys, lw['mla_wvb'], bp, lp, tq=_pick(lp, 256, 8))
        o_mla_s = _mla_sample(mla_q, mla_rows, cache_mla_t, page_table, lw['mla_wvb'], l, mp)
        o_conv_p, conv_p = _conv_mix(p_conv, 0, lp, zeros(bp, CONV_W - 1, GROUP_W), lw, tl=_pick(lp, 512, 8))
        o_conv_s, conv_s = _conv_mix(p_conv, mp, ls, state_conv[l], lw, tl=ls)
        x2, qx = _token_call(
            _mix_out_kernel,
            [x1, both(o_rw_p, o_rw_s), both(o_nsa_p, o_nsa_s), both(o_mla_p, o_mla_s), both(o_conv_p, o_conv_s)],
            [lw['w_out'], n[2], lw['x_wq']], [d, GROUP_W], [F32] * 2, tm)
        ox = both(_cross_attend(qx, 0, lp, mem_kv, 1, _pick(lp, 512, 8)),
                  _cross_attend(qx, mp, ls, cache_mem[l], _pick(bs, 8, 1), ls))
        x3, h2 = _token_call(functools.partial(_xo_ffn_a_kernel, chunk=chunk), [x2, ox],
                             [lw['x_wo'], n[4], *ffn_w(1)], [d, d_ff], [F32, BF16], tm)
        (x,) = _token_call(_ffn_b_kernel, [x3, h2], [lw['ffn_wd'][1]], [d], [F32], tm)

        nsa_rows = p_nsa[:, GROUP_W:GROUP_W + 4 * HEAD_DIM]
        keep = min(WINDOW, lp)
        nsa_win_p = p_nsa[:mp, GROUP_W + 4 * HEAD_DIM:GROUP_W + 6 * HEAD_DIM].reshape(bp, lp, 2 * HEAD_DIM)[:, lp - keep:]
        sp.append((mla_rows[:mp].reshape(bp, lp, C_MLA), nsa_rows[:mp].reshape(bp, lp, 4 * HEAD_DIM), nsa_win_p,
                   mem_kv, rw_state_p, p_rw[:mp].reshape(bp, lp, RW_COLS)[:, -1], conv_p))
        ss.append((mla_rows[mp:].reshape(bs, ls, C_MLA), nsa_rows[mp:].reshape(bs, ls, 4 * HEAD_DIM), win_s,
                   rw_state_s, p_rw[mp:].reshape(bs, ls, RW_COLS)[:, -1], conv_s))

    (y,) = _token_call(_norm_kernel, [x], [final_norm[None, :]], [d], [F32], tm)
    stack = lambda states, i: jnp.stack([s[i] for s in states])
    return (y[:mp].reshape(bp, lp, d), y[mp:].reshape(bs, ls, d),
            *(stack(sp, i) for i in range(7)), *(stack(ss, i) for i in range(6)))
```

```python
import functools

import jax
import jax.numpy as jnp
from jax import lax
from jax.experimental import pallas as pl
from jax.experimental.pallas import tpu as pltpu

F32 = jnp.float32
BF16 = jnp.bfloat16
HIGHEST = lax.Precision.HIGHEST

GROUP_W = 256
HEAD_DIM = 64
N_HEADS = 4
LANES = 128
SUBLANES = 8
W_LORA, A_LORA, G_LORA = 32, 32, 64
RW_COLS = 3 * GROUP_W + W_LORA + A_LORA + G_LORA
RW_GN_EPS = 64e-5
CMP_LEN, CMP_STRIDE, CMP_HID = 32, 16, 128
SLC_BLOCK, N_SELECT, WINDOW = 64, 16, 512
KV_LORA, QK_NOPE, QK_ROPE = 128, 64, 32
C_MLA = KV_LORA + QK_ROPE
MLA_SCALE = (QK_NOPE + QK_ROPE) ** -0.5
ROPE_THETA = 10000.0
CONV_W = 31
CONV_LN_EPS = 1e-5
NORM_EPS = 1e-6
NEG = -1e30
FORCE_SCORE = 1e4
NSA_W = 768
MLA_W = 512
VMEM_LIMIT = 56 * 1024 * 1024


def _cparams(sem, vmem=VMEM_LIMIT):
    return pltpu.CompilerParams(dimension_semantics=sem, vmem_limit_bytes=vmem)


def _rms(x, g):
    return x * lax.rsqrt(jnp.mean(x * x, axis=-1, keepdims=True) + NORM_EPS) * g


def _bdot(a, b):
    return jnp.dot(a.astype(BF16), b.astype(BF16), preferred_element_type=F32)


def _bdot_t(a, b):
    return lax.dot_general(a.astype(BF16), b.astype(BF16), (((1,), (1,)), ((), ())), preferred_element_type=F32)


def _seg_ones(n, seg):
    r = lax.broadcasted_iota(jnp.int32, (n, n), 0) // seg
    c = lax.broadcasted_iota(jnp.int32, (n, n), 1) // seg
    return (r == c).astype(F32)


def _full(shape):
    nd = len(shape)
    return pl.BlockSpec(shape, lambda *_: (0,) * nd)


def _swiglu_to(xn, wg_ref, wu_ref, h_ref, chunk):
    for c in range(h_ref.shape[1] // chunk):
        sl = slice(c * chunk, (c + 1) * chunk)
        gate = jnp.dot(xn, wg_ref[:, sl], preferred_element_type=F32)
        up = jnp.dot(xn, wu_ref[:, sl], preferred_element_type=F32)
        h_ref[:, sl] = (gate * jax.nn.sigmoid(gate) * up).astype(BF16)


def _ffn_a_kernel(x_ref, g_ref, wg_ref, wu_ref, h_ref, *, chunk):
    xn = _rms(x_ref[...], g_ref[...]).astype(BF16)
    _swiglu_to(xn, wg_ref, wu_ref, h_ref, chunk)


def _ffn_b_proj_kernel(x_ref, h_ref, wd_ref, g_ref, win_ref, x1_ref, prw_ref, pnsa_ref, pmla_ref, pconv_ref):
    x1 = x_ref[...] + 0.5 * jnp.dot(h_ref[...], wd_ref[...], preferred_element_type=F32)
    x1_ref[...] = x1
    xn = _rms(x1, g_ref[...]).astype(BF16)
    off = 0
    for ref in (prw_ref, pnsa_ref, pmla_ref, pconv_ref):
        w = ref.shape[1]
        ref[...] = jnp.dot(xn, win_ref[:, off:off + w], preferred_element_type=F32)
        off += w


def _mix_out_kernel(x_ref, orw_ref, onsa_ref, omla_ref, oconv_ref, wo_ref, g_ref, wq_ref, x2_ref, q_ref):
    acc = x_ref[...]
    for i, ref in enumerate((orw_ref, onsa_ref, omla_ref, oconv_ref)):
        acc = acc + _bdot(ref[...], wo_ref[i * GROUP_W:(i + 1) * GROUP_W, :])
    x2_ref[...] = acc
    q_ref[...] = _bdot(_rms(acc, g_ref[...]), wq_ref[...]) * (HEAD_DIM ** -0.5)


def _xo_ffn_a_kernel(x_ref, ox_ref, wo_ref, g_ref, wg_ref, wu_ref, x3_ref, h_ref, *, chunk):
    x3 = x_ref[...] + _bdot(ox_ref[...], wo_ref[...])
    x3_ref[...] = x3
    _swiglu_to(_rms(x3, g_ref[...]).astype(BF16), wg_ref, wu_ref, h_ref, chunk)


def _ffn_b_kernel(x_ref, h_ref, wd_ref, o_ref):
    o_ref[...] = x_ref[...] + 0.5 * jnp.dot(h_ref[...], wd_ref[...], preferred_element_type=F32)


def _norm_kernel(x_ref, g_ref, o_ref):
    o_ref[...] = _rms(x_ref[...], g_ref[...])


def _norm_proj_kernel(x_ref, g_ref, w_ref, o_ref):
    o_ref[...] = _bdot(_rms(x_ref[...], g_ref[...]), w_ref[...])


def _token_call(body, tiled_in, full_in, out_widths, out_dtypes, tm):
    m = tiled_in[0].shape[0]
    assert m % tm == 0
    in_specs = [pl.BlockSpec((tm, a.shape[1]), lambda i: (i, 0)) for a in tiled_in]
    in_specs += [_full(a.shape) for a in full_in]
    out_specs = [pl.BlockSpec((tm, w), lambda i: (i, 0)) for w in out_widths]
    out_shape = [jax.ShapeDtypeStruct((m, w), dt) for w, dt in zip(out_widths, out_dtypes)]
    return pl.pallas_call(body, grid=(m // tm,), in_specs=in_specs, out_specs=out_specs, out_shape=out_shape,
                          compiler_params=_cparams(("parallel",)))(*tiled_in, *full_in)


def _ff_chunk(d_ff):
    return 256 if d_ff % 256 == 0 else LANES


def _xattn_kernel(q_ref, kv_ref, o_ref, *, groups, lq):
    for g in range(groups):
        rows = slice(g * lq, (g + 1) * lq)
        for h in range(N_HEADS):
            cols = slice(h * HEAD_DIM, (h + 1) * HEAD_DIM)
            k = kv_ref[g, :, cols]
            v = kv_ref[g, :, GROUP_W + h * HEAD_DIM:GROUP_W + (h + 1) * HEAD_DIM]
            s = _bdot_t(q_ref[rows, cols], k)
            e = jnp.exp(s - jnp.max(s, axis=-1, keepdims=True))
            p = e / jnp.sum(e, axis=-1, keepdims=True)
            o_ref[rows, cols] = _bdot(p, v)


def _xattn_stacked_kernel(q_ref, kv_ref, o_ref, *, groups, lq):
    head = lax.broadcasted_iota(jnp.int32, (lq, GROUP_W), 1) // HEAD_DIM
    for g in range(groups):
        rows = slice(g * lq, (g + 1) * lq)
        q = q_ref[rows, :]
        qs = jnp.concatenate([jnp.where(head == h, q, 0.0) for h in range(N_HEADS)], axis=0)
        s = _bdot_t(qs, kv_ref[g, :, 0:GROUP_W])
        e = jnp.exp(s - jnp.max(s, axis=-1, keepdims=True))
        p = e / jnp.sum(e, axis=-1, keepdims=True)
        o = _bdot(p, kv_ref[g, :, GROUP_W:2 * GROUP_W])
        out = jnp.zeros((lq, GROUP_W), F32)
        for h in range(N_HEADS):
            out = jnp.where(head == h, o[h * lq:(h + 1) * lq], out)
        o_ref[rows, :] = out


def _cross_attend(q, row0, l, mem_kv, groups, lq_tile):
    b, n_mem, _ = mem_kv.shape
    if groups > 1:
        assert lq_tile == l and b % groups == 0 and row0 % (groups * l) == 0
        blk0 = row0 // (groups * l)
        grid = (b // groups,)
        q_spec = pl.BlockSpec((groups * l, GROUP_W), lambda i: (blk0 + i, 0))
        o_spec = pl.BlockSpec((groups * l, GROUP_W), lambda i: (i, 0))
        kv_spec = pl.BlockSpec((groups, n_mem, 2 * GROUP_W), lambda i: (i, 0, 0))
        body = _xattn_stacked_kernel
    else:
        assert l % lq_tile == 0 and row0 % lq_tile == 0
        nq = l // lq_tile
        blk0 = row0 // lq_tile
        grid = (b, nq)
        q_spec = pl.BlockSpec((lq_tile, GROUP_W), lambda bi, i: (blk0 + bi * nq + i, 0))
        o_spec = pl.BlockSpec((lq_tile, GROUP_W), lambda bi, i: (bi * nq + i, 0))
        kv_spec = pl.BlockSpec((1, n_mem, 2 * GROUP_W), lambda bi, i: (bi, 0, 0))
        body = _xattn_kernel
    return pl.pallas_call(
        functools.partial(body, groups=groups, lq=lq_tile), grid=grid, in_specs=[q_spec, kv_spec],
        out_specs=o_spec, out_shape=jax.ShapeDtypeStruct((b * l, GROUP_W), F32),
        compiler_params=_cparams(("parallel",) * len(grid)))(q, mem_kv)


def _rw_prep_kernel(p_ref, prev_ref, mu_ref, vec_ref, w2_ref, a2_ref, g2_ref,
                    rp_ref, nkk_ref, w_ref, kka_ref, k_ref, v_ref, ovk_ref, g_ref, bonus_ref, carry_ref, *, seqs):
    p = p_ref[...]
    tl = p.shape[0]
    row = lax.broadcasted_iota(jnp.int32, p.shape, 0)
    if seqs == 1:
        @pl.when(pl.program_id(1) == 0)
        def _():
            carry_ref[...] = prev_ref[0]

        prev = jnp.where(row == 0, carry_ref[...], pltpu.roll(p, 1, axis=0))
        carry_ref[...] = p[tl - 1:tl, :]
    else:
        seq = tl // seqs
        first = jnp.broadcast_to(prev_ref[...], (seqs, seq, p.shape[1])).reshape(tl, p.shape[1])
        prev = jnp.where(row % seq == 0, first, pltpu.roll(p, 1, axis=0))
    xs = p + (prev - p) * mu_ref[...]
    r, k, v = xs[:, 0:GROUP_W], xs[:, GROUP_W:2 * GROUP_W], xs[:, 2 * GROUP_W:3 * GROUP_W]
    o = 3 * GROUP_W
    xw, xa, xg = xs[:, o:o + W_LORA], xs[:, o + W_LORA:o + W_LORA + A_LORA], xs[:, o + W_LORA + A_LORA:]
    w0, a0, k_k, k_a, r_k = (vec_ref[i:i + 1, :] for i in range(5))
    z = -(w0 + _bdot(jnp.tanh(xw), w2_ref[...]))
    softplus = jnp.maximum(z, 0.0) + jnp.log1p(jnp.exp(-jnp.abs(z)))
    decay = jnp.exp(-jnp.exp(-softplus - 0.5))
    a = jax.nn.sigmoid(a0 + _bdot(xa, a2_ref[...]))
    g_ref[...] = _bdot(jax.nn.sigmoid(xg), g2_ref[...])
    ones = _seg_ones(GROUP_W, HEAD_DIM)
    seg = lambda t: jnp.dot(t, ones, precision=HIGHEST, preferred_element_type=F32)
    kk = k * k_k
    kk = kk / jnp.maximum(jnp.sqrt(seg(kk * kk)), 1e-12)
    k_eff = k * (1.0 + (a - 1.0) * k_a)
    nkk = -kk
    kka = kk * a
    rp_ref[...] = decay * r + nkk * seg(kka * r)
    ovk_ref[...] = v * seg(k_eff * r)
    bonus_ref[...] = seg(r * k_eff * r_k) * v
    nkk_ref[...] = nkk
    w_ref[...] = decay
    kka_ref[...] = kka
    k_ref[...] = k_eff
    v_ref[...] = v


def _rw_scan_kernel(rp_ref, nkk_ref, w_ref, kka_ref, k_ref, v_ref, ovk_ref, g_ref, bonus_ref, s0_ref, vec_ref,
                    out_ref, sfin_ref, s_ref, o_ref, *, nb, tc):
    j = pl.program_id(1)

    @pl.when(j == 0)
    def _():
        s_ref[...] = s0_ref[...]

    pairs = [(b, p) for b in range(nb) for p in range(N_HEADS // 2)]
    left1 = lax.broadcasted_iota(jnp.int32, (1, 1, LANES), 2) < HEAD_DIM
    left = lax.broadcasted_iota(jnp.int32, (1, HEAD_DIM, LANES), 2) < HEAD_DIM
    eye2 = (lax.broadcasted_iota(jnp.int32, (1, HEAD_DIM, LANES), 1)
            == lax.broadcasted_iota(jnp.int32, (1, HEAD_DIM, LANES), 2) % HEAD_DIM).astype(F32)
    rsum = lambda t: jnp.sum(t, axis=2, keepdims=True)
    head_ones = _seg_ones(LANES, HEAD_DIM).astype(BF16)
    eye2b = eye2.astype(BF16)

    def halves(rw):
        return jnp.where(left1, rw, 0.0), jnp.where(left1, 0.0, rw)

    def head_sums(x):
        y = jnp.dot(x.reshape(len(pairs) * HEAD_DIM, LANES).astype(BF16), head_ones, preferred_element_type=F32)
        return y.reshape(len(pairs), HEAD_DIM, LANES)

    def bf16_pieces(x):
        hi = x.astype(BF16).astype(F32)
        mid = (x - hi).astype(BF16).astype(F32)
        return hi, mid, x - hi - mid

    def step(t8, carry):
        base = pl.multiple_of(t8 * SUBLANES, SUBLANES)
        ld = lambda ref: jnp.stack([ref[b, pl.ds(base, SUBLANES), p * LANES:(p + 1) * LANES] for b, p in pairs])
        nkk, rp, v, w, kka, k = (ld(r) for r in (nkk_ref, rp_ref, v_ref, w_ref, kka_ref, k_ref))
        s = s_ref[...].reshape(len(pairs), HEAD_DIM, LANES)
        v3 = [piece.astype(BF16) for piece in bf16_pieces(v)]
        vcs = [sum(head_sums(eye2b * piece[:, i:i + 1, :]) for piece in v3) for i in range(SUBLANES)]
        o_rows = []
        for i in range(SUBLANES):
            row = lambda x: x[:, i:i + 1, :]
            nkk_l, nkk_r = halves(row(nkk))
            sa = jnp.where(left, rsum(s * nkk_l), rsum(s * nkk_r))
            oc = head_sums(s * row(rp))
            s = s * row(w) + sa * row(kka) + vcs[i] * row(k)
            o_rows.append(jnp.sum(oc * eye2, axis=1, keepdims=True))
        s_ref[...] = s.reshape(s_ref.shape)
        o_new = jnp.concatenate(o_rows, axis=1) + ld(ovk_ref)
        for n, (b, p) in enumerate(pairs):
            o_ref[b, pl.ds(base, SUBLANES), p * LANES:(p + 1) * LANES] = o_new[n]
        return carry

    lax.fori_loop(0, tc // SUBLANES, step, 0)

    @pl.when(j == pl.num_programs(1) - 1)
    def _():
        sfin_ref[...] = s_ref[...]

    mean = _seg_ones(GROUP_W, HEAD_DIM) * (1.0 / HEAD_DIM)
    seg_mean = lambda t: jnp.dot(t, mean, precision=HIGHEST, preferred_element_type=F32)
    ln_w, ln_b = vec_ref[5:6, :], vec_ref[6:7, :]
    for b in range(nb):
        o = o_ref[b]
        d = o - seg_mean(o)
        on = d * lax.rsqrt(seg_mean(d * d) + RW_GN_EPS) * ln_w + ln_b
        out_ref[b] = (on + bonus_ref[b]) * g_ref[b]


def _rwkv_mix(p, row0, l, p_prev, s0, lw, nb, tc, tl):
    b = p_prev.shape[0]
    assert l % tc == 0 and b % nb == 0 and tc % SUBLANES == 0 and row0 % tl == 0
    seqs_per_tile = max(tl // l, 1)
    assert (l % tl == 0) if seqs_per_tile == 1 else (tl % l == 0 and b % seqs_per_tile == 0)
    nt = max(l // tl, 1)
    blk0 = row0 // tl
    tile = pl.BlockSpec((tl, GROUP_W), lambda bi, j: (bi * nt + j, 0))
    outs = pl.pallas_call(
        functools.partial(_rw_prep_kernel, seqs=seqs_per_tile), grid=(b // seqs_per_tile, nt),
        in_specs=[pl.BlockSpec((tl, RW_COLS), lambda bi, j: (blk0 + bi * nt + j, 0)),
                  pl.BlockSpec((seqs_per_tile, 1, RW_COLS), lambda bi, j: (bi, 0, 0)),
                  _full((1, RW_COLS)), _full((7, GROUP_W)), _full((W_LORA, GROUP_W)), _full((A_LORA, GROUP_W)),
                  _full((G_LORA, GROUP_W))],
        out_specs=[tile] * 9, out_shape=[jax.ShapeDtypeStruct((b * l, GROUP_W), F32)] * 9,
        scratch_shapes=[pltpu.VMEM((1, RW_COLS), F32)],
        compiler_params=_cparams(("arbitrary", "arbitrary")))(
            p, p_prev[:, None, :], lw['rw_mu'], lw['rw_vec'], lw['rw_w2'], lw['rw_a2'], lw['rw_g2'])
    seqs = [a.reshape(b, l, GROUP_W) for a in outs]
    s0p = s0.reshape(b, 2, 2, HEAD_DIM, HEAD_DIM).transpose(0, 1, 3, 2, 4).reshape(b, 2, HEAD_DIM, LANES)
    seq_spec = pl.BlockSpec((nb, tc, GROUP_W), lambda bi, j: (bi, j, 0))
    st_spec = pl.BlockSpec((nb, 2, HEAD_DIM, LANES), lambda bi, j: (bi, 0, 0, 0))
    out, sfin = pl.pallas_call(
        functools.partial(_rw_scan_kernel, nb=nb, tc=tc), grid=(b // nb, l // tc),
        in_specs=[seq_spec] * 9 + [st_spec, _full((7, GROUP_W))],
        out_specs=[seq_spec, st_spec],
        out_shape=[jax.ShapeDtypeStruct((b, l, GROUP_W), F32), jax.ShapeDtypeStruct(s0p.shape, F32)],
        scratch_shapes=[pltpu.VMEM((nb, 2, HEAD_DIM, LANES), F32), pltpu.VMEM((nb, tc, GROUP_W), F32)],
        compiler_params=_cparams(("arbitrary", "arbitrary")))(*seqs, s0p, lw['rw_vec'])
    sfin = sfin.reshape(b, 2, HEAD_DIM, 2, HEAD_DIM).transpose(0, 1, 3, 2, 4).reshape(b, N_HEADS, HEAD_DIM, HEAD_DIM)
    return out.reshape(b * l, GROUP_W), sfin


HALO = 32


def _conv_kernel(u_ref, buf_ref, dw_ref, vec_ref, pw_ref, y_ref, st_ref, hp_ref, *, tl):
    j = pl.program_id(1)

    @pl.when(j == 0)
    def _():
        hp_ref[0:HALO - (CONV_W - 1), :] = jnp.zeros((HALO - (CONV_W - 1), GROUP_W), F32)
        hp_ref[HALO - (CONV_W - 1):HALO, :] = buf_ref[...]

    @pl.when(j > 0)
    def _():
        hp_ref[0:HALO, :] = hp_ref[tl:tl + HALO, :]

    u = u_ref[...]
    hp_ref[HALO:HALO + tl, :] = u[:, 0:GROUP_W] * jax.nn.sigmoid(u[:, GROUP_W:])
    acc = jnp.zeros((tl, GROUP_W), F32)
    for i in range(CONV_W):
        o = HALO - (CONV_W - 1) + i
        acc = acc + hp_ref[o:o + tl, :] * dw_ref[i:i + 1, :]
    y = acc + vec_ref[0:1, :]
    d = y - jnp.mean(y, axis=-1, keepdims=True)
    y = d * lax.rsqrt(jnp.mean(d * d, axis=-1, keepdims=True) + CONV_LN_EPS) * vec_ref[1:2, :] + vec_ref[2:3, :]
    y_ref[...] = _bdot(y * jax.nn.sigmoid(y), pw_ref[...])

    @pl.when(j == pl.num_programs(1) - 1)
    def _():
        st_ref[...] = hp_ref[tl + HALO - (CONV_W - 1):tl + HALO, :]


def _conv_mix(u, row0, l, buf, lw, tl):
    b = buf.shape[0]
    assert l % tl == 0 and row0 % tl == 0
    nt = l // tl
    blk0 = row0 // tl
    return pl.pallas_call(
        functools.partial(_conv_kernel, tl=tl), grid=(b, nt),
        in_specs=[pl.BlockSpec((tl, 2 * GROUP_W), lambda bi, j: (blk0 + bi * nt + j, 0)),
                  pl.BlockSpec((None, CONV_W - 1, GROUP_W), lambda bi, j: (bi, 0, 0)),
                  _full((CONV_W, GROUP_W)), _full((3, GROUP_W)), _full((GROUP_W, GROUP_W))],
        out_specs=[pl.BlockSpec((tl, GROUP_W), lambda bi, j: (bi * nt + j, 0)),
                   pl.BlockSpec((None, CONV_W - 1, GROUP_W), lambda bi, j: (bi, 0, 0))],
        out_shape=[jax.ShapeDtypeStruct((b * l, GROUP_W), F32), jax.ShapeDtypeStruct(buf.shape, F32)],
        scratch_shapes=[pltpu.VMEM((tl + HALO, GROUP_W), F32)],
        compiler_params=_cparams(("arbitrary", "arbitrary")))(u, buf, lw['conv_dw'], lw['conv_vec'], lw['conv_pw'])


def _mla_prep_kernel(p_ref, cos_ref, sin_ref, qn_ref, kvn_ref, wqn_ref, wqr_ref, wqp_ref, wkb_ref,
                     rows_ref, keys_ref, q_ref):
    p = p_ref[...]
    cos, sin = cos_ref[...], sin_ref[...]
    c_q = _rms(p[:, 0:LANES], qn_ref[...]).astype(BF16)
    c_kv = _rms(p[:, LANES:2 * LANES], kvn_ref[...])
    k_rope = p[:, 2 * LANES:3 * LANES] * cos + p[:, 3 * LANES:4 * LANES] * sin
    rows_ref[:, 0:KV_LORA] = c_kv
    rows_ref[:, KV_LORA:C_MLA] = k_rope[:, 0:QK_ROPE]
    keys_ref[...] = jnp.concatenate([c_kv, k_rope], axis=-1).astype(BF16)
    for h in range(N_HEADS):
        q_nope = jnp.dot(c_q, wqn_ref[h], preferred_element_type=F32)
        q_lat = _bdot(q_nope, wkb_ref[h])
        q_rope = (jnp.dot(c_q, wqr_ref[h], preferred_element_type=F32) * cos
                  + jnp.dot(c_q, wqp_ref[h], preferred_element_type=F32) * sin)
        q_ref[h] = jnp.concatenate([q_lat, q_rope], axis=-1) * MLA_SCALE


def _mla_prep(p_mla, cos, sin, lw, tm):
    m = p_mla.shape[0]
    assert m % tm == 0
    row = lambda w: pl.BlockSpec((tm, w), lambda i: (i, 0))
    return pl.pallas_call(
        _mla_prep_kernel, grid=(m // tm,),
        in_specs=[row(MLA_W), row(LANES), row(LANES), _full((1, LANES)), _full((1, LANES)),
                  _full((N_HEADS, LANES, QK_NOPE)), _full((N_HEADS, LANES, LANES)), _full((N_HEADS, LANES, LANES)),
                  _full((N_HEADS, QK_NOPE, KV_LORA))],
        out_specs=[row(C_MLA), row(2 * LANES), pl.BlockSpec((N_HEADS, tm, 2 * LANES), lambda i: (0, i, 0))],
        out_shape=[jax.ShapeDtypeStruct((m, C_MLA), F32), jax.ShapeDtypeStruct((m, 2 * LANES), BF16),
                   jax.ShapeDtypeStruct((N_HEADS, m, 2 * LANES), F32)],
        compiler_params=_cparams(("parallel",)))(
            p_mla, cos, sin, lw['mla_q_norm'], lw['mla_kv_norm'], lw['mla_wq_nope'], lw['mla_wq_rope'],
            lw['mla_wq_rope_rot'], lw['mla_wkb'])


def _mla_out(o_lat, wvb_ref, rows_per_head):
    out = None
    for h in range(N_HEADS):
        t = _bdot(o_lat[h * rows_per_head:(h + 1) * rows_per_head], wvb_ref[h])
        out = t if out is None else out + t
    return out


def _mla_prompt_kernel(q_ref, keys_ref, wvb_ref, o_ref, *, tq):
    i = pl.program_id(1)
    q = q_ref[...].reshape(N_HEADS * tq, 2 * LANES).astype(BF16)
    t = i * tq + lax.broadcasted_iota(jnp.int32, (N_HEADS * tq, tq), 0) % tq
    col = lax.broadcasted_iota(jnp.int32, (N_HEADS * tq, tq), 1)

    def body(kt, carry, diagonal=False):
        m, l, acc = carry
        k = keys_ref[pl.ds(pl.multiple_of(kt * tq, tq), tq), :]
        s = _bdot_t(q, k)
        if diagonal:
            mask = kt * tq + col <= t
            s = jnp.where(mask, s, NEG)
        m_new = jnp.maximum(m, jnp.max(s, axis=-1, keepdims=True))
        e = jnp.exp(s - m_new)
        if diagonal:
            e = jnp.where(mask, e, 0.0)
        alpha = jnp.exp(m - m_new)
        return m_new, alpha * l + jnp.sum(e, axis=-1, keepdims=True), alpha * acc + _bdot(e, k[:, 0:KV_LORA])

    init = (jnp.full((N_HEADS * tq, 1), NEG, F32), jnp.zeros((N_HEADS * tq, 1), F32),
            jnp.zeros((N_HEADS * tq, KV_LORA), F32))
    m, l, acc = body(i, lax.fori_loop(0, i, body, init), diagonal=True)
    o_ref[...] = _mla_out(acc / l, wvb_ref, tq)


def _mla_prompt(q, keys, wvb, b, l, tq):
    m = b * l
    assert l % tq == 0
    nq = l // tq
    return pl.pallas_call(
        functools.partial(_mla_prompt_kernel, tq=tq), grid=(b, nq),
        in_specs=[pl.BlockSpec((N_HEADS, tq, 2 * LANES), lambda bi, i: (0, bi * nq + i, 0)),
                  pl.BlockSpec((l, 2 * LANES), lambda bi, i: (bi, 0)), _full(wvb.shape)],
        out_specs=pl.BlockSpec((tq, GROUP_W), lambda bi, i: (bi * nq + i, 0)),
        out_shape=jax.ShapeDtypeStruct((m, GROUP_W), F32),
        compiler_params=_cparams(("parallel", "parallel")))(q, keys, wvb)


def _page_copies(table_ref, windows, sem_ref, b, slot, n_pages):
    return [pltpu.make_async_copy(src, dst, sem_ref.at[slot])
            for j in range(n_pages) for src, dst in windows(table_ref[b, j], slot, j)]


def _paged_fetch(table_ref, windows, sem_ref, n_pages):
    b = pl.program_id(0)
    slot = b % 2

    @pl.when(b == 0)
    def _():
        for c in _page_copies(table_ref, windows, sem_ref, b, slot, n_pages):
            c.start()

    @pl.when(b + 1 < pl.num_programs(0))
    def _():
        for c in _page_copies(table_ref, windows, sem_ref, b + 1, 1 - slot, n_pages):
            c.start()

    for c in _page_copies(table_ref, windows, sem_ref, b, slot, n_pages):
        c.wait()
    return slot


def _mla_sample_kernel(table_ref, q_ref, new_ref, pool_ref, wvb_ref, o_ref, buf_ref, sem_ref, *, layer, n_pages, lq):
    page = buf_ref.shape[2] // n_pages
    windows = lambda pid, slot, j: [(pool_ref.at[layer, pid], buf_ref.at[slot, :, pl.ds(j * page, page)])]
    slot = _paged_fetch(table_ref, windows, sem_ref, n_pages)
    past_t = buf_ref[slot].astype(BF16)
    new = new_ref[...].astype(BF16)
    rows = N_HEADS * lq
    q = q_ref[...].reshape(rows, 2 * LANES)[:, 0:C_MLA].astype(BF16)
    s_past = jnp.dot(q, past_t, preferred_element_type=F32)
    qi = lax.broadcasted_iota(jnp.int32, (rows, lq), 0) % lq
    mask = lax.broadcasted_iota(jnp.int32, (rows, lq), 1) <= qi
    s_new = jnp.where(mask, _bdot_t(q, new), NEG)
    m = jnp.maximum(jnp.max(s_past, axis=-1, keepdims=True), jnp.max(s_new, axis=-1, keepdims=True))
    e_past = jnp.exp(s_past - m)
    e_new = jnp.where(mask, jnp.exp(s_new - m), 0.0)
    den = jnp.sum(e_past, axis=-1, keepdims=True) + jnp.sum(e_new, axis=-1, keepdims=True)
    o_lat = (_bdot_t(e_past, past_t[0:KV_LORA, :]) + _bdot(e_new, new[:, 0:KV_LORA])) / den
    o_ref[...] = _mla_out(o_lat, wvb_ref, lq)


def _mla_sample(q, new_rows, pool_t, page_table, wvb, layer, row0):
    b, n_pages = page_table.shape
    page = pool_t.shape[3]
    lq = (q.shape[1] - row0) // b
    assert row0 % lq == 0 and page % LANES == 0
    blk0 = row0 // lq
    grid_spec = pltpu.PrefetchScalarGridSpec(
        num_scalar_prefetch=1, grid=(b,),
        in_specs=[pl.BlockSpec((N_HEADS, lq, 2 * LANES), lambda i, tbl: (0, blk0 + i, 0)),
                  pl.BlockSpec((lq, C_MLA), lambda i, tbl: (blk0 + i, 0)),
                  pl.BlockSpec(memory_space=pl.ANY),
                  pl.BlockSpec(wvb.shape, lambda i, tbl: (0, 0, 0))],
        out_specs=pl.BlockSpec((lq, GROUP_W), lambda i, tbl: (i, 0)),
        scratch_shapes=[pltpu.VMEM((2, C_MLA, n_pages * page), F32), pltpu.SemaphoreType.DMA((2,))])
    return pl.pallas_call(
        functools.partial(_mla_sample_kernel, layer=layer, n_pages=n_pages, lq=lq), grid_spec=grid_spec,
        out_shape=jax.ShapeDtypeStruct((b * lq, GROUP_W), F32),
        compiler_params=_cparams(("arbitrary",)))(page_table, q, new_rows, pool_t, wvb)


def _gelu_tanh(x):
    return 0.5 * x * (1.0 + jnp.tanh(0.7978845608028654 * (x + 0.044715 * x * x * x)))


def _compress(a, wexp_ref, pos_ref, b1_ref, w2k_ref, w2v_ref, b2_ref):
    n_ch = a.shape[0]
    c = jnp.dot(pos_ref[...], wexp_ref[...], preferred_element_type=F32)
    up = lambda t: pltpu.roll(t, n_ch - 1, axis=0)
    hk = a[:, 0:LANES] + up(a[:, LANES:2 * LANES]) + c[0:1, 0:LANES] + c[1:2, LANES:2 * LANES] + b1_ref[:, 0:LANES]
    hv = (a[:, 2 * LANES:3 * LANES] + up(a[:, 3 * LANES:4 * LANES]) + c[0:1, 2 * LANES:3 * LANES]
          + c[1:2, 3 * LANES:4 * LANES] + b1_ref[:, LANES:2 * LANES])
    out = _bdot(_gelu_tanh(hk), w2k_ref[...]) + _bdot(_gelu_tanh(hv), w2v_ref[...]) + b2_ref[...]
    valid = lax.broadcasted_iota(jnp.int32, out.shape, 0) < n_ch - 1
    return jnp.where(valid, out, 0.0)


def _compress_kernel(ch_ref, wexp_ref, pos_ref, b1_ref, w2k_ref, w2v_ref, b2_ref, o_ref):
    a = jnp.dot(ch_ref[...].astype(BF16), wexp_ref[...], preferred_element_type=F32)
    o_ref[...] = _compress(a, wexp_ref, pos_ref, b1_ref, w2k_ref, w2v_ref, b2_ref)


def _stack_heads(q, scale):
    low = lax.broadcasted_iota(jnp.int32, (q.shape[0], LANES), 1) < HEAD_DIM
    parts = []
    for h in range(N_HEADS):
        slab = q[:, (h // 2) * LANES:(h // 2 + 1) * LANES]
        if h % 2:
            slab = pltpu.roll(slab, HEAD_DIM, axis=1)
        parts.append(jnp.where(low, slab * scale, 0.0))
    return jnp.concatenate(parts, axis=0).astype(BF16)


def _softmax_parts(parts):
    m = None
    for s, mask in parts:
        mx = jnp.max(jnp.where(mask, s, NEG), axis=-1, keepdims=True)
        m = mx if m is None else jnp.maximum(m, mx)
    es = [jnp.where(mask, jnp.exp(jnp.where(mask, s, NEG) - m), 0.0) for s, mask in parts]
    den = sum(jnp.sum(e, axis=-1, keepdims=True) for e in es)
    return es, jnp.where(den > 0.0, den, 1.0)


def _select_blocks(p_c, t, n_cb, n_sb, nq, fillers=()):
    ncbp = p_c.shape[1]
    nsbp = -(-n_sb // LANES) * LANES
    psum = p_c[0:nq] + p_c[nq:2 * nq] + p_c[2 * nq:3 * nq] + p_c[3 * nq:4 * nq]
    ci = lax.broadcasted_iota(jnp.int32, (ncbp, nsbp), 0)
    sj = lax.broadcasted_iota(jnp.int32, (ncbp, nsbp), 1)
    overlap = ((CMP_STRIDE * ci < SLC_BLOCK * (sj + 1)) & (CMP_STRIDE * ci + CMP_LEN > SLC_BLOCK * sj)
               & (ci < n_cb) & (sj < n_sb)).astype(BF16)
    hi = psum.astype(BF16)
    lo = (psum - hi.astype(F32)).astype(BF16)
    imp = jnp.dot(hi, overlap, preferred_element_type=F32) + jnp.dot(lo, overlap, preferred_element_type=F32)
    j = lax.broadcasted_iota(jnp.int32, (nq, nsbp), 1)
    cur = t // SLC_BLOCK
    forced = (j == 0) | (j == cur) | (j == cur - 1)
    score = jnp.where(forced, FORCE_SCORE, jnp.where(j <= cur, imp, -1.0))
    score = jnp.where(j < n_sb, score, -3e38)
    sel = jnp.zeros((nq, nsbp), F32)
    fillers = list(fillers)
    filled = []
    rounds = min(N_SELECT, n_sb)
    for r in range(rounds):
        mx = jnp.max(score, axis=-1, keepdims=True)
        first = jnp.min(jnp.where(score == mx, j, nsbp), axis=-1, keepdims=True)
        hit = j == first
        sel = jnp.where(hit, 1.0, sel)
        score = jnp.where(hit, -3e38, score)
        share = -(-len(fillers) // (rounds - r))
        filled += [f() for f in fillers[:share]]
        fillers = fillers[share:]
    return sel.astype(BF16), filled


def _tile4(x):
    return jnp.concatenate([x] * N_HEADS, axis=0)


def _gate_combine(gates, o_c, o_s, o_w, o_ref, nq):
    for h in range(N_HEADS):
        rows = slice(h * nq, (h + 1) * nq)
        g = lambda c: gates[:, 3 * h + c:3 * h + c + 1]
        o = g(0) * o_c[rows] + g(1) * o_s[rows] + g(2) * o_w[rows]
        o_ref[:, h * HEAD_DIM:(h + 1) * HEAD_DIM] = o[:, HEAD_DIM:LANES]


def _nsa_prompt_kernel(q_ref, g_ref, cmp_ref, slc_ref, win_ref, o_ref, slc_s, win_s, *, tq, l, tks):
    i = pl.program_id(1)

    @pl.when(i == 0)
    def _():
        slc_s[...] = slc_ref[...].astype(BF16)
        win_s[...] = win_ref[...].astype(BF16)

    rows = N_HEADS * tq
    n_cb, n_sb = l // CMP_STRIDE - 1, l // SLC_BLOCK
    q = _stack_heads(q_ref[...], HEAD_DIM ** -0.5)
    tq1 = i * tq + lax.broadcasted_iota(jnp.int32, (tq, 1), 0)
    t = _tile4(tq1)
    kcvc = cmp_ref[...].astype(BF16)
    n = lax.broadcasted_iota(jnp.int32, (rows, kcvc.shape[0]), 1)
    (e_c,), den = _softmax_parts([(_bdot_t(q, kcvc), (CMP_STRIDE * n + CMP_LEN <= t + 1) & (n < n_cb))])
    p_c = e_c / den
    span = WINDOW + tq
    w0 = pl.multiple_of(jnp.maximum(i * tq - WINDOW, 0), tq)
    wpos = w0 + lax.broadcasted_iota(jnp.int32, (tq, span), 1)
    wmask = (wpos <= tq1) & (wpos > tq1 - WINDOW)

    def window_head(h):
        kvw = win_s[pl.ds(w0, span), :]
        (e_w,), den_w = _softmax_parts([(_bdot_t(q[h * tq:(h + 1) * tq], kvw), wmask)])
        return _bdot(e_w, kvw) / den_w

    fillers = [functools.partial(window_head, h) for h in range(N_HEADS)] + [lambda: _bdot(p_c, kcvc)]
    sel, (*o_w, o_c) = _select_blocks(p_c, tq1, n_cb, n_sb, tq, fillers)
    o_w = jnp.concatenate(o_w, axis=0)
    blk = lax.broadcasted_iota(jnp.int32, (sel.shape[1], tks), 0)
    kcol = lax.broadcasted_iota(jnp.int32, (sel.shape[1], tks), 1)
    col = lax.broadcasted_iota(jnp.int32, (rows, tks), 1)

    def body(kt, carry):
        m, lsum, acc = carry
        k0 = pl.multiple_of(kt * tks, tks)
        kv = slc_s[pl.ds(k0, tks), :]
        expand = (blk == (k0 + kcol) // SLC_BLOCK).astype(BF16)
        chosen = _tile4(jnp.dot(sel, expand, preferred_element_type=F32)) > 0.5
        mask = chosen & (k0 + col <= t)
        s = jnp.where(mask, _bdot_t(q, kv), NEG)
        m_new = jnp.maximum(m, jnp.max(s, axis=-1, keepdims=True))
        e = jnp.where(mask, jnp.exp(s - m_new), 0.0)
        alpha = jnp.exp(m - m_new)
        return m_new, alpha * lsum + jnp.sum(e, axis=-1, keepdims=True), alpha * acc + _bdot(e, kv)

    init = (jnp.full((rows, 1), NEG, F32), jnp.zeros((rows, 1), F32), jnp.zeros((rows, LANES), F32))
    _, lsum, acc = lax.fori_loop(0, ((i + 1) * tq + tks - 1) // tks, body, init)
    o_s = acc / jnp.where(lsum > 0.0, lsum, 1.0)
    _gate_combine(jax.nn.sigmoid(g_ref[...]), o_c, o_s, o_w, o_ref, tq)


def _nsa_prompt(p_nsa, lw, b, l, tq):
    m = b * l
    n_ch = l // CMP_STRIDE
    tks = 8 * SLC_BLOCK
    assert l % tq == 0 and l >= WINDOW + tq and l % tks == 0 and tq % SLC_BLOCK == 0 and WINDOW % tq == 0
    ch = p_nsa[:m, GROUP_W:GROUP_W + LANES].reshape(m // CMP_STRIDE, CMP_STRIDE * LANES)
    kcvc = pl.pallas_call(
        _compress_kernel, grid=(b,),
        in_specs=[pl.BlockSpec((n_ch, CMP_STRIDE * LANES), lambda bi: (bi, 0)), _full(lw['cmp_wexp_p'].shape),
                  _full(lw['cmp_pos_p'].shape), _full((1, 2 * LANES)), _full((LANES, LANES)), _full((LANES, LANES)),
                  _full((1, LANES))],
        out_specs=pl.BlockSpec((n_ch, LANES), lambda bi: (bi, 0)),
        out_shape=jax.ShapeDtypeStruct((b * n_ch, LANES), F32),
        compiler_params=_cparams(("parallel",)))(
            ch, lw['cmp_wexp_p'], lw['cmp_pos_p'], lw['cmp_b1'], lw['cmp_w2k'], lw['cmp_w2v'], lw['cmp_b2'])
    nq = l // tq
    slab = lambda c: pl.BlockSpec((l, LANES), lambda bi, i: (bi, c))
    return pl.pallas_call(
        functools.partial(_nsa_prompt_kernel, tq=tq, l=l, tks=tks), grid=(b, nq),
        in_specs=[pl.BlockSpec((tq, GROUP_W), lambda bi, i: (bi * nq + i, 0)),
                  pl.BlockSpec((tq, LANES), lambda bi, i: (bi * nq + i, NSA_W // LANES - 1)),
                  pl.BlockSpec((n_ch, LANES), lambda bi, i: (bi, 0)), slab(3), slab(4)],
        out_specs=pl.BlockSpec((tq, GROUP_W), lambda bi, i: (bi * nq + i, 0)),
        out_shape=jax.ShapeDtypeStruct((m, GROUP_W), F32),
        scratch_shapes=[pltpu.VMEM((l, LANES), BF16), pltpu.VMEM((l, LANES), BF16)],
        compiler_params=_cparams(("parallel", "arbitrary")))(p_nsa, p_nsa, kcvc, p_nsa, p_nsa)


def _nsa_sample_kernel(table_ref, q_ref, g_ref, nslc_ref, nwin_ref, win_ref, pool_ref, wexp_ref, pos_ref, b1_ref,
                       w2k_ref, w2v_ref, b2_ref, o_ref, wout_ref, cmp_buf, slc_buf, sem_ref, *, layer, n_pages, lq, past,
                       tks):
    page = past // n_pages

    def windows(pid, slot, j):
        return [(pool_ref.at[layer, pid, :, pl.ds(c * LANES, LANES)], buf.at[slot, pl.ds(j * page, page), :])
                for c, buf in enumerate((cmp_buf, slc_buf))]

    slot = _paged_fetch(table_ref, windows, sem_ref, n_pages)
    n_ch = past // CMP_STRIDE
    rows = N_HEADS * lq
    n_cb = (past + lq) // CMP_STRIDE - 1
    n_sb = -(-(past + lq) // SLC_BLOCK)
    q = _stack_heads(q_ref[...], HEAD_DIM ** -0.5)
    tq1 = past + lax.broadcasted_iota(jnp.int32, (lq, 1), 0)
    t = _tile4(tq1)
    a = jnp.zeros((n_ch, 4 * CMP_HID), F32)
    for r in range(0, CMP_STRIDE, 2):
        x = jnp.concatenate([cmp_buf[slot, pl.ds(r + d, n_ch, stride=CMP_STRIDE), :].astype(BF16) for d in range(2)],
                            axis=1)
        a = a + jnp.dot(x, wexp_ref[r * LANES:(r + 2) * LANES, :], preferred_element_type=F32)
    kcvc = _compress(a, wexp_ref, pos_ref, b1_ref, w2k_ref, w2v_ref, b2_ref).astype(BF16)
    n = lax.broadcasted_iota(jnp.int32, (rows, n_ch), 1)
    (e_c,), den = _softmax_parts([(_bdot_t(q, kcvc), (CMP_STRIDE * n + CMP_LEN <= t + 1) & (n < n_cb))])
    p_c = e_c / den
    qi = lax.broadcasted_iota(jnp.int32, (rows, lq), 0) % lq
    causal_new = lax.broadcasted_iota(jnp.int32, (rows, lq), 1) <= qi

    def window():
        win = win_ref[...]
        nwin = nwin_ref[...]
        wb = win.shape[0]
        wpos = past - wb + lax.broadcasted_iota(jnp.int32, (rows, wb), 1)
        es, den_w = _softmax_parts([(_bdot_t(q, win), (wpos <= t) & (wpos > t - WINDOW) & (wpos >= 0)),
                                    (_bdot_t(q, nwin), causal_new)])
        wout_ref[0:wb - lq, :] = win[lq:wb, :]
        wout_ref[wb - lq:wb, :] = nwin
        return (_bdot(es[0], win) + _bdot(es[1], nwin)) / den_w

    def slc_scores(kt):
        kv = slc_buf[slot, pl.ds(kt * tks, tks), :].astype(BF16)
        return kv, _bdot_t(q, kv)

    fillers = ([window, lambda: _bdot(p_c, kcvc)] + [functools.partial(slc_scores, kt) for kt in range(past // tks)])
    sel, (o_w, o_c, *tiles) = _select_blocks(p_c, tq1, n_cb, n_sb, lq, fillers)
    nsbp = sel.shape[1]
    blk = lax.broadcasted_iota(jnp.int32, (nsbp, tks), 0)
    kcol = lax.broadcasted_iota(jnp.int32, (nsbp, tks), 1)
    parts = []
    for kt, (kv, s) in enumerate(tiles):
        expand = (blk == (kt * tks + kcol) // SLC_BLOCK).astype(BF16)
        parts.append((s, _tile4(jnp.dot(sel, expand, preferred_element_type=F32)) > 0.5))
    nslc = nslc_ref[...].astype(BF16)
    new_blk = past // SLC_BLOCK
    sel_new = _tile4(sel[:, new_blk:new_blk + 1].astype(F32)) > 0.5
    parts.append((_bdot_t(q, nslc), causal_new & sel_new))
    es, den = _softmax_parts(parts)
    acc = _bdot(es[-1], nslc)
    for e, (kv, _) in zip(es[:-1], tiles):
        acc = acc + _bdot(e, kv)
    o_s = acc / den
    _gate_combine(jax.nn.sigmoid(g_ref[...]), o_c, o_s, o_w, o_ref, lq)


def _nsa_sample(p_nsa, row0, pool, page_table, win_cache, lw, layer):
    b, n_pages = page_table.shape
    page, c_in = pool.shape[2:]
    past = n_pages * page
    lq = (p_nsa.shape[0] - row0) // b
    wb = win_cache.shape[1]
    tks = min(past, 1024)
    assert row0 % lq == 0 and page % CMP_STRIDE == 0 and lq < CMP_STRIDE and lq % 8 == 0 and past % tks == 0
    blk0 = row0 // lq
    row = lambda w, c: pl.BlockSpec((lq, w), lambda i, tbl: (blk0 + i, c))
    const = lambda a: pl.BlockSpec(a.shape, lambda i, tbl: (0,) * a.ndim)
    weights = [lw['cmp_wexp_p'], lw['cmp_pos_p'], lw['cmp_b1'], lw['cmp_w2k'], lw['cmp_w2v'], lw['cmp_b2']]
    grid_spec = pltpu.PrefetchScalarGridSpec(
        num_scalar_prefetch=1, grid=(b,),
        in_specs=[row(GROUP_W, 0), row(LANES, NSA_W // LANES - 1), row(LANES, 3), row(LANES, 4),
                  pl.BlockSpec((None, wb, LANES), lambda i, tbl: (i, 0, 0)),
                  pl.BlockSpec(memory_space=pl.ANY)] + [const(w) for w in weights],
        out_specs=[pl.BlockSpec((lq, GROUP_W), lambda i, tbl: (i, 0)),
                   pl.BlockSpec((None, wb, LANES), lambda i, tbl: (i, 0, 0))],
        scratch_shapes=[pltpu.VMEM((2, past, LANES), F32), pltpu.VMEM((2, past, LANES), F32),
                        pltpu.SemaphoreType.DMA((2,))])
    return pl.pallas_call(
        functools.partial(_nsa_sample_kernel, layer=layer, n_pages=n_pages, lq=lq, past=past, tks=tks),
        grid_spec=grid_spec,
        out_shape=[jax.ShapeDtypeStruct((b * lq, GROUP_W), F32), jax.ShapeDtypeStruct(win_cache.shape, F32)],
        compiler_params=_cparams(("arbitrary",)))(page_table, p_nsa, p_nsa, p_nsa, p_nsa, win_cache, pool, *weights)


def _pad_cols(w, width):
    return jnp.pad(w, ((0, 0), (0, width - w.shape[1])))


def _rot_cols(w):
    half = QK_ROPE // 2
    return jnp.concatenate([-w[:, half:], w[:, :half]], axis=1)


def _expand_cmp_w1(w1, channels):
    half = CMP_STRIDE * HEAD_DIM
    out = jnp.zeros((CMP_STRIDE, channels, 4 * CMP_HID), F32)
    for br in range(2):
        for part in range(2):
            blk = w1[br, part * half:(part + 1) * half].reshape(CMP_STRIDE, HEAD_DIM, CMP_HID)
            c0 = (2 * br + part) * CMP_HID
            out = out.at[:, br * HEAD_DIM:(br + 1) * HEAD_DIM, c0:c0 + CMP_HID].set(blk)
    return out.reshape(CMP_STRIDE * channels, 4 * CMP_HID).astype(BF16)


def _expand_cmp_pos(pos, channels):
    out = jnp.zeros((8, CMP_STRIDE, channels), F32)
    for br in range(2):
        for part in range(2):
            out = out.at[part, :, br * HEAD_DIM:(br + 1) * HEAD_DIM].set(
                pos[br, part * CMP_STRIDE:(part + 1) * CMP_STRIDE])
    return out.reshape(8, CMP_STRIDE * channels).astype(BF16)


def _layer_weights(w, l):
    nsa0 = RW_COLS
    mla0 = nsa0 + GROUP_W + 6 * HEAD_DIM + 3 * N_HEADS
    conv0 = mla0 + 2 * LANES + QK_ROPE
    w_in = w['w_in'][l]
    k_rope = w_in[:, mla0 + 2 * LANES:conv0]
    w_in_ext = jnp.concatenate([
        w_in[:, :nsa0], _pad_cols(w_in[:, nsa0:mla0], NSA_W), w_in[:, mla0:mla0 + 2 * LANES],
        _pad_cols(k_rope, LANES), _pad_cols(_rot_cols(k_rope), LANES), w_in[:, conv0:]], axis=1).astype(BF16)
    d_hd = QK_NOPE + QK_ROPE
    wqb = w['mla_w_qb'][l].reshape(-1, N_HEADS, d_hd).transpose(1, 0, 2)
    wq_rope = wqb[:, :, QK_NOPE:]
    pad3 = lambda a: jnp.pad(a, ((0, 0), (0, 0), (0, LANES - a.shape[2]))).astype(BF16)
    wvb = jnp.zeros((N_HEADS, KV_LORA, GROUP_W), F32)
    for h in range(N_HEADS):
        wvb = wvb.at[h, :, h * HEAD_DIM:(h + 1) * HEAD_DIM].set(w['mla_w_vb'][l][:, h, :])
    w2 = w['cmp_w2'][l]
    return dict(
        norms=w['norms'][l][:, None, :],
        ffn_wg=w['ffn_w_gate'][l].astype(BF16), ffn_wu=w['ffn_w_up'][l].astype(BF16),
        ffn_wd=w['ffn_w_down'][l].astype(BF16),
        w_in_ext=w_in_ext, w_out=w['w_out'][l].astype(BF16),
        rw_mu=w['rw_mu'][l][None, :], rw_vec=w['rw_vec'][l], rw_w2=w['rw_w2'][l].astype(BF16),
        rw_a2=w['rw_a2'][l].astype(BF16), rw_g2=w['rw_g2'][l].astype(BF16),
        cmp_wexp_p=_expand_cmp_w1(w['cmp_w1'][l], LANES), cmp_pos_p=_expand_cmp_pos(w['cmp_pos'][l], LANES),
        cmp_b1=w['cmp_b1'][l].reshape(1, 2 * CMP_HID),
        cmp_w2k=_pad_cols(w2[0], LANES).astype(BF16),
        cmp_w2v=jnp.pad(w2[1], ((0, 0), (HEAD_DIM, 0))).astype(BF16),
        cmp_b2=w['cmp_b2'][l].reshape(1, 2 * HEAD_DIM),
        mla_q_norm=w['mla_q_norm'][l][None, :], mla_kv_norm=w['mla_kv_norm'][l][None, :],
        mla_wq_nope=wqb[:, :, :QK_NOPE].astype(BF16), mla_wq_rope=pad3(wq_rope),
        mla_wq_rope_rot=pad3(jnp.concatenate([-wq_rope[:, :, QK_ROPE // 2:], wq_rope[:, :, :QK_ROPE // 2]], axis=2)),
        mla_wkb=w['mla_w_kb'][l].transpose(1, 2, 0).astype(BF16), mla_wvb=wvb.astype(BF16),
        conv_dw=w['conv_dw'][l], conv_vec=w['conv_vec'][l], conv_pw=w['conv_pw'][l].astype(BF16),
        x_wq=w['x_wq'][l].astype(BF16), x_wkv=w['x_wkv'][l].astype(BF16), x_wo=w['x_wo'][l].astype(BF16))


def _rope_tables(pos):
    half = QK_ROPE // 2
    inv = ROPE_THETA ** (-jnp.arange(half, dtype=F32) / half)
    ang = pos.astype(F32)[:, None] * inv
    tile = lambda a: jnp.tile(a, (1, LANES // half))
    return tile(jnp.cos(ang)), tile(jnp.sin(ang))


def _pick(n, *cands):
    for c in cands:
        if n % c == 0:
            return c
    return n


def kernel(x_prompt, x_sample, cache_mla, cache_nsa, cache_nsa_win, cache_mem, state_rwkv, state_rwkv_shift,
           state_conv, page_table, mem_prompt, norms, ffn_w_gate, ffn_w_up, ffn_w_down, w_in, w_out, rw_mu, rw_vec,
           rw_w2, rw_a2, rw_g2, cmp_pos, cmp_w1, cmp_b1, cmp_w2, cmp_b2, mla_q_norm, mla_kv_norm, mla_w_qb, mla_w_kb,
           mla_w_vb, conv_dw, conv_vec, conv_pw, x_wq, x_wkv, x_wo, final_norm):
    w = dict(norms=norms, ffn_w_gate=ffn_w_gate, ffn_w_up=ffn_w_up, ffn_w_down=ffn_w_down, w_in=w_in, w_out=w_out,
             rw_mu=rw_mu, rw_vec=rw_vec, rw_w2=rw_w2, rw_a2=rw_a2, rw_g2=rw_g2, cmp_pos=cmp_pos, cmp_w1=cmp_w1,
             cmp_b1=cmp_b1, cmp_w2=cmp_w2, cmp_b2=cmp_b2, mla_q_norm=mla_q_norm, mla_kv_norm=mla_kv_norm,
             mla_w_qb=mla_w_qb, mla_w_kb=mla_w_kb, mla_w_vb=mla_w_vb, conv_dw=conv_dw, conv_vec=conv_vec,
             conv_pw=conv_pw, x_wq=x_wq, x_wkv=x_wkv, x_wo=x_wo)
    depth = norms.shape[0]
    bp, lp, d = x_prompt.shape
    bs, ls, _ = x_sample.shape
    mp, ms = bp * lp, bs * ls
    m = mp + ms
    n_pages, page = page_table.shape[1], cache_mla.shape[2]
    past = n_pages * page
    d_ff = ffn_w_gate.shape[-1]
    chunk = _ff_chunk(d_ff)
    tm = _pick(m, 512, 256, 128, 64, 8)
    n_mem = mem_prompt.shape[1]

    x = jnp.concatenate([x_prompt.reshape(mp, d), x_sample.reshape(ms, d)], axis=0)
    cos_p, sin_p = _rope_tables(jnp.arange(lp, dtype=jnp.int32))
    cos_s, sin_s = _rope_tables(past + jnp.arange(ls, dtype=jnp.int32))
    cos = jnp.concatenate([jnp.tile(cos_p, (bp, 1)), jnp.tile(cos_s, (bs, 1))], axis=0)
    sin = jnp.concatenate([jnp.tile(sin_p, (bp, 1)), jnp.tile(sin_s, (bs, 1))], axis=0)
    mem_rows = mem_prompt.reshape(bp * n_mem, d)
    cache_mla_t = jnp.swapaxes(cache_mla, 2, 3)
    zeros = lambda *s: jnp.zeros(s, F32)
    sp, ss = [], []
    both = lambda a, c: jnp.concatenate([a, c], axis=0)

    for l in range(depth):
        lw = _layer_weights(w, l)
        n = lw['norms']
        ffn_w = lambda i: (lw['ffn_wg'][i], lw['ffn_wu'][i])
        (h1,) = _token_call(functools.partial(_ffn_a_kernel, chunk=chunk), [x], [n[0], *ffn_w(0)], [d_ff], [BF16], tm)
        x1, p_rw, p_nsa, p_mla, p_conv = _token_call(
            _ffn_b_proj_kernel, [x, h1], [lw['ffn_wd'][0], n[1], lw['w_in_ext']],
            [d, RW_COLS, NSA_W, MLA_W, 2 * GROUP_W], [F32] * 5, tm)
        (mem_kv,) = _token_call(_norm_proj_kernel, [mem_rows], [n[3], lw['x_wkv']], [2 * GROUP_W], [F32],
                                _pick(bp * n_mem, 256, 8))
        mem_kv = mem_kv.reshape(bp, n_mem, 2 * GROUP_W)
        o_rw_p, rw_state_p = _rwkv_mix(p_rw, 0, lp, zeros(bp, RW_COLS), zeros(bp, N_HEADS, HEAD_DIM, HEAD_DIM), lw,
                                       nb=bp, tc=_pick(lp, 256, 8), tl=_pick(lp, 512, 8))
        o_rw_s, rw_state_s = _rwkv_mix(p_rw, mp, ls, state_rwkv_shift[l], state_rwkv[l], lw,
                                       nb=_pick(bs, 4, 1), tc=ls, tl=_pick(ms, 512, 64, ls))
        o_nsa_p = _nsa_prompt(p_nsa, lw, bp, lp, tq=_pick(lp, 256, 128))
        o_nsa_s, win_s = _nsa_sample(p_nsa, mp, cache_nsa, page_table, cache_nsa_win[l], lw, l)
        mla_rows, mla_keys, mla_q = _mla_prep(p_mla, cos, sin, lw, tm)
        o_mla_p = _mla_prompt(mla_q, mla_keys, lw['mla_wvb'], bp, lp, tq=_pick(lp, 512, 256, 8))
        o_mla_s = _mla_sample(mla_q, mla_rows, cache_mla_t, page_table, lw['mla_wvb'], l, mp)
        o_conv_p, conv_p = _conv_mix(p_conv, 0, lp, zeros(bp, CONV_W - 1, GROUP_W), lw, tl=_pick(lp, 512, 8))
        o_conv_s, conv_s = _conv_mix(p_conv, mp, ls, state_conv[l], lw, tl=ls)
        x2, qx = _token_call(
            _mix_out_kernel,
            [x1, both(o_rw_p, o_rw_s), both(o_nsa_p, o_nsa_s), both(o_mla_p, o_mla_s), both(o_conv_p, o_conv_s)],
            [lw['w_out'], n[2], lw['x_wq']], [d, GROUP_W], [F32] * 2, tm)
        ox = both(_cross_attend(qx, 0, lp, mem_kv, 1, _pick(lp, 512, 8)),
                  _cross_attend(qx, mp, ls, cache_mem[l], _pick(bs, 8, 1), ls))
        x3, h2 = _token_call(functools.partial(_xo_ffn_a_kernel, chunk=chunk), [x2, ox],
                             [lw['x_wo'], n[4], *ffn_w(1)], [d, d_ff], [F32, BF16], tm)
        (x,) = _token_call(_ffn_b_kernel, [x3, h2], [lw['ffn_wd'][1]], [d], [F32], tm)

        nsa_rows = p_nsa[:, GROUP_W:GROUP_W + 4 * HEAD_DIM]
        keep = min(WINDOW, lp)
        nsa_win_p = p_nsa[:mp, GROUP_W + 4 * HEAD_DIM:GROUP_W + 6 * HEAD_DIM].reshape(bp, lp, 2 * HEAD_DIM)[:, lp - keep:]
        sp.append((mla_rows[:mp].reshape(bp, lp, C_MLA), nsa_rows[:mp].reshape(bp, lp, 4 * HEAD_DIM), nsa_win_p,
                   mem_kv, rw_state_p, p_rw[:mp].reshape(bp, lp, RW_COLS)[:, -1], conv_p))
        ss.append((mla_rows[mp:].reshape(bs, ls, C_MLA), nsa_rows[mp:].reshape(bs, ls, 4 * HEAD_DIM), win_s,
                   rw_state_s, p_rw[mp:].reshape(bs, ls, RW_COLS)[:, -1], conv_s))

    (y,) = _token_call(_norm_kernel, [x], [final_norm[None, :]], [d], [F32], tm)
    stack = lambda states, i: jnp.stack([s[i] for s in states])
    return (y[:mp].reshape(bp, lp, d), y[mp:].reshape(bs, ls, d),
            *(stack(sp, i) for i in range(7)), *(stack(ss, i) for i in range(6)))
```
